```python
import jax, jax.numpy as jnp
from jax import lax
import numpy as np

D_MODEL = 1024
BATCH = 4
SEQ = 4096
DEPTH = 4
DEC_BATCH = 32
DEC_SEQ = 1
PAST_LEN = 8192
PAGE_SIZE = 128

H_RET = 4
DK_RET = 128
DV_RET = 256
RET_CHUNK = 128
W_RET = H_RET * DV_RET
H_FOX = 8
HD_FOX = 128
W_FOX = H_FOX * HD_FOX
Q_BLOCK = 128
FORGET_BIAS = 7.0
D_PLE = 256
ROPE_BASE = 10000.0
EPS = 1e-6
PROJ_SIZES = (H_RET * DK_RET, H_RET * DK_RET, W_RET, W_RET, W_FOX, W_FOX, W_FOX, H_FOX, W_FOX, D_MODEL, D_MODEL)
IN_WIDTH = 2 * H_RET * DK_RET + 2 * W_RET + 4 * W_FOX + H_FOX + 2 * D_MODEL

kernel_name = 'hybrid_retention_fox_decoder_step'

F32 = jnp.float32


def rmsnorm(x, g):
    xf = x.astype(F32)
    y = xf * lax.rsqrt(jnp.mean(xf * xf, axis=-1, keepdims=True) + EPS)
    return (y * g.astype(F32)).astype(x.dtype)


def rotary(x, pos):
    half = x.shape[-1] // 2
    inv = ROPE_BASE ** (-jnp.arange(half, dtype=F32) / half)
    ang = pos.astype(F32)[:, None] * inv[None, :]
    cos = jnp.cos(ang)[None, :, None, :]
    sin = jnp.sin(ang)[None, :, None, :]
    x1 = x[..., :half].astype(F32)
    x2 = x[..., half:].astype(F32)
    return jnp.concatenate([x1 * cos - x2 * sin, x2 * cos + x1 * sin], axis=-1).astype(x.dtype)


def ret_log_gamma():
    return jnp.log1p(-jnp.exp2(-5.0 - jnp.arange(H_RET, dtype=F32)))


def split_projection(z):
    offs = np.cumsum((0,) + PROJ_SIZES)
    return [z[..., int(offs[i]):int(offs[i + 1])] for i in range(len(PROJ_SIZES))]


def project(x, pos, norm_g, w_in, b_forget, q_norm_g, k_norm_g):
    B, T, _ = x.shape
    z = rmsnorm(x, norm_g) @ w_in
    qr, kr, vr, gr, qf, kf, vf, fl, gf, ga, gb = split_projection(z)
    qr = rotary(qr.reshape(B, T, H_RET, DK_RET), pos)
    kr = rotary(kr.reshape(B, T, H_RET, DK_RET), pos) * (DK_RET ** -0.5)
    vr = vr.reshape(B, T, H_RET, DV_RET)
    qf = rmsnorm(qf.reshape(B, T, H_FOX, HD_FOX), q_norm_g)
    kf = rmsnorm(kf.reshape(B, T, H_FOX, HD_FOX), k_norm_g)
    vf = vf.reshape(B, T, H_FOX, HD_FOX)
    logf = jax.nn.log_sigmoid(fl.astype(F32) + b_forget.astype(F32))
    return qr, kr, vr, gr, qf, kf, vf, logf, gf, ga, gb


def retention_chunk(q, k, v, s_prev, log_gamma):
    L = q.shape[1]
    idx = jnp.arange(L, dtype=F32)
    rel = idx[:, None] - idx[None, :]
    decay = jnp.where(rel >= 0, jnp.exp(log_gamma[:, None, None] * jnp.maximum(rel, 0.0)[None]), 0.0)
    qf, kf, vf = q.astype(F32), k.astype(F32), v.astype(F32)
    scores = jnp.einsum('bthd,bshd->bhts', qf, kf) * decay[None]
    inner = jnp.einsum('bhts,bshv->bthv', scores, vf)
    q_decay = jnp.exp(log_gamma[None, :] * (idx[:, None] + 1.0))
    cross = jnp.einsum('bthd,bhdv->bthv', qf, s_prev) * q_decay[None, :, :, None]
    k_decay = jnp.exp(log_gamma[None, :] * (L - 1.0 - idx)[:, None])
    s_new = jnp.exp(log_gamma * L)[None, :, None, None] * s_prev + jnp.einsum('bshd,bshv,sh->bhdv', kf, vf, k_decay)
    return inner + cross, s_new


def retention_prompt(q, k, v, log_gamma):
    B, S, H, DK = q.shape
    nc = S // RET_CHUNK

    def to_chunks(a):
        return a.reshape((B, nc, RET_CHUNK) + a.shape[2:]).swapaxes(0, 1)

    s0 = jnp.zeros((B, H, DK, v.shape[-1]), F32)

    def step(s, xs):
        qc, kc, vc = xs
        o, s = retention_chunk(qc, kc, vc, s, log_gamma)
        return s, o

    s_fin, o = lax.scan(step, s0, (to_chunks(q), to_chunks(k), to_chunks(v)))
    return o.swapaxes(0, 1).reshape(B, S, H, v.shape[-1]), s_fin


def head_groupnorm(o, g):
    B, T = o.shape[0], o.shape[1]
    mu = jnp.mean(o, axis=-1, keepdims=True)
    d = o - mu
    y = d * lax.rsqrt(jnp.mean(d * d, axis=-1, keepdims=True) + EPS)
    return y.reshape(B, T, -1) * g.astype(F32)


def fox_prompt(q, k, v, logf):
    B, S, H, D = q.shape
    nb = S // Q_BLOCK
    F = jnp.cumsum(logf, axis=1)
    FT = F.transpose(0, 2, 1)
    kf, vf = k.astype(F32), v.astype(F32)
    qb = q.reshape(B, nb, Q_BLOCK, H, D).swapaxes(0, 1)
    Fb = F.reshape(B, nb, Q_BLOCK, H).swapaxes(0, 1)
    kpos = jnp.arange(S)
    scale = D ** -0.5

    def block(args):
        i, qi, Fi = args
        s = jnp.einsum('bthd,bshd->bhts', qi.astype(F32), kf) * scale
        s = s + Fi.transpose(0, 2, 1)[:, :, :, None] - FT[:, :, None, :]
        qpos = i * Q_BLOCK + jnp.arange(Q_BLOCK)
        s = jnp.where((kpos[None, :] <= qpos[:, None])[None, None], s, -jnp.inf)
        p = jax.nn.softmax(s, axis=-1)
        return jnp.einsum('bhts,bshd->bthd', p, vf)

    o = lax.map(block, (jnp.arange(nb), qb, Fb))
    return o.swapaxes(0, 1).reshape(B, S, H, D)


def fox_sample(q, k_new, v_new, logf_new, k_past, v_past, logf_past):
    T, D = q.shape[1], q.shape[-1]
    P = k_past.shape[1]
    scale = D ** -0.5
    qf = q.astype(F32)
    Fn = jnp.cumsum(logf_new, axis=1).transpose(0, 2, 1)
    suffix = (lax.cumsum(logf_past, axis=1, reverse=True) - logf_past).transpose(0, 2, 1)
    s_past = jnp.einsum('bthd,bshd->bhts', qf, k_past.astype(F32)) * scale + Fn[:, :, :, None] + suffix[:, :, None, :]
    s_new = jnp.einsum('bthd,bshd->bhts', qf, k_new.astype(F32)) * scale + Fn[:, :, :, None] - Fn[:, :, None, :]
    causal = jnp.tril(jnp.ones((T, T), dtype=bool))
    s_new = jnp.where(causal[None, None], s_new, -jnp.inf)
    p = jax.nn.softmax(jnp.concatenate([s_past, s_new], axis=-1), axis=-1)
    return (jnp.einsum('bhts,bshd->bthd', p[..., :P], v_past.astype(F32))
            + jnp.einsum('bhts,bshd->bthd', p[..., P:], v_new.astype(F32)))


def combine(h, p, o_r, o_f, g_r, g_f, g_a, g_b, ret_gn_g, w_branch_ret, w_branch_fox, w_out, w_ple_gate, w_ple_proj):
    dt = h.dtype
    B, T, _ = h.shape
    a = (head_groupnorm(o_r, ret_gn_g).astype(dt) * jax.nn.silu(g_r)) @ w_branch_ret
    b = (o_f.reshape(B, T, W_FOX).astype(dt) * jax.nn.silu(g_f)) @ w_branch_fox
    m = jax.nn.sigmoid(g_a) * a + jax.nn.sigmoid(g_b) * b
    h = h + m @ w_out
    return h + jax.nn.sigmoid(h @ w_ple_gate) * (p.astype(dt) @ w_ple_proj)


def setup_inputs(seed: int = 0) -> dict:
    key = jax.random.key(seed)
    ks = jax.random.split(key, 24)
    n_pages = PAST_LEN // PAGE_SIZE
    n_used = DEC_BATCH * n_pages
    n_pool = (n_used * 5) // 4
    nrm = jax.random.normal
    page_table = jax.random.permutation(ks[0], n_pool)[:n_used].reshape(DEC_BATCH, n_pages).astype(jnp.int32)
    return {
        'x_prompt': nrm(ks[1], (BATCH, SEQ, D_MODEL), F32),
        'x_sample': nrm(ks[2], (DEC_BATCH, DEC_SEQ, D_MODEL), F32),
        'cache_k': nrm(ks[3], (DEPTH, n_pool, PAGE_SIZE, H_FOX, HD_FOX), F32),
        'cache_v': nrm(ks[4], (DEPTH, n_pool, PAGE_SIZE, H_FOX, HD_FOX), F32),
        'cache_logf': jax.nn.log_sigmoid(FORGET_BIAS + 0.5 * nrm(ks[5], (DEPTH, n_pool, PAGE_SIZE, H_FOX), F32)),
        'state_ret': 0.5 * nrm(ks[6], (DEPTH, DEC_BATCH, H_RET, DK_RET, DV_RET), F32),
        'page_table': page_table,
        'p_prompt': nrm(ks[7], (DEPTH, BATCH, SEQ, D_PLE), F32),
        'p_sample': nrm(ks[8], (DEPTH, DEC_BATCH, DEC_SEQ, D_PLE), F32),
        'norm_g': 1.0 + 0.02 * nrm(ks[9], (DEPTH, D_MODEL), F32),
        'w_in': nrm(ks[10], (DEPTH, D_MODEL, IN_WIDTH), F32) * D_MODEL ** -0.5,
        'b_forget': FORGET_BIAS + 0.5 * nrm(ks[11], (DEPTH, H_FOX), F32),
        'q_norm_g': 1.0 + 0.02 * nrm(ks[12], (DEPTH, HD_FOX), F32),
        'k_norm_g': 1.0 + 0.02 * nrm(ks[13], (DEPTH, HD_FOX), F32),
        'ret_gn_g': 1.0 + 0.02 * nrm(ks[14], (DEPTH, W_RET), F32),
        'w_branch_ret': nrm(ks[15], (DEPTH, W_RET, D_MODEL), F32) * W_RET ** -0.5,
        'w_branch_fox': nrm(ks[16], (DEPTH, W_FOX, D_MODEL), F32) * W_FOX ** -0.5,
        'w_out': nrm(ks[17], (DEPTH, D_MODEL, D_MODEL), F32) * D_MODEL ** -0.5,
        'w_ple_gate': nrm(ks[18], (DEPTH, D_MODEL, D_MODEL), F32) * D_MODEL ** -0.5,
        'w_ple_proj': nrm(ks[19], (DEPTH, D_PLE, D_MODEL), F32) * D_PLE ** -0.5,
    }


def reference(x_prompt, x_sample, cache_k, cache_v, cache_logf, state_ret, page_table, p_prompt, p_sample,
              norm_g, w_in, b_forget, q_norm_g, k_norm_g, ret_gn_g, w_branch_ret, w_branch_fox, w_out,
              w_ple_gate, w_ple_proj):
    lg = ret_log_gamma()
    DB, T = x_sample.shape[0], x_sample.shape[1]
    past_len = page_table.shape[1] * cache_k.shape[2]
    pos_p = jnp.arange(x_prompt.shape[1])
    pos_s = past_len + jnp.arange(T)
    hp, hs = x_prompt, x_sample
    kp_l, vp_l, fp_l, sp_l = [], [], [], []
    ks_l, vs_l, fs_l, ss_l = [], [], [], []
    for l in range(DEPTH):
        qr, kr, vr, gr, qf, kf, vf, logf, gf, ga, gb = project(hp, pos_p, norm_g[l], w_in[l], b_forget[l], q_norm_g[l], k_norm_g[l])
        o_r, s_r = retention_prompt(qr, kr, vr, lg)
        o_f = fox_prompt(qf, kf, vf, logf)
        hp = combine(hp, p_prompt[l], o_r, o_f, gr, gf, ga, gb, ret_gn_g[l], w_branch_ret[l], w_branch_fox[l], w_out[l], w_ple_gate[l], w_ple_proj[l])
        kp_l.append(kf); vp_l.append(vf); fp_l.append(logf.astype(cache_logf.dtype)); sp_l.append(s_r.astype(state_ret.dtype))
        qr, kr, vr, gr, qf, kf, vf, logf, gf, ga, gb = project(hs, pos_s, norm_g[l], w_in[l], b_forget[l], q_norm_g[l], k_norm_g[l])
        o_r, s_r = retention_chunk(qr, kr, vr, state_ret[l].astype(F32), lg)
        k_past = cache_k[l][page_table].reshape(DB, past_len, H_FOX, HD_FOX)
        v_past = cache_v[l][page_table].reshape(DB, past_len, H_FOX, HD_FOX)
        f_past = cache_logf[l][page_table].reshape(DB, past_len, H_FOX).astype(F32)
        o_f = fox_sample(qf, kf, vf, logf, k_past, v_past, f_past)
        hs = combine(hs, p_sample[l], o_r, o_f, gr, gf, ga, gb, ret_gn_g[l], w_branch_ret[l], w_branch_fox[l], w_out[l], w_ple_gate[l], w_ple_proj[l])
        ks_l.append(kf); vs_l.append(vf); fs_l.append(logf.astype(cache_logf.dtype)); ss_l.append(s_r.astype(state_ret.dtype))
    return (hp, hs, jnp.stack(kp_l), jnp.stack(vp_l), jnp.stack(fp_l), jnp.stack(sp_l),
            jnp.stack(ks_l), jnp.stack(vs_l), jnp.stack(fs_l), jnp.stack(ss_l))
```

```python
import functools
import math

import numpy as np
import jax
import jax.numpy as jnp
from jax import lax
from jax.experimental import pallas as pl
from jax.experimental.pallas import tpu as pltpu

F32 = jnp.float32
BF16 = jnp.bfloat16

H_RET = 4
DK_RET = 128
DV_RET = 256
RET_CHUNK = 128
W_RET = H_RET * DV_RET
H_FOX = 8
HD_FOX = 128
W_FOX = H_FOX * HD_FOX
ROPE_BASE = 10000.0
EPS = 1e-6
LANES = 128
NEG_BIG = -1e30
VMEM_LIMIT = 56 * 1024 * 1024

_QR0 = 0
_KR0 = _QR0 + H_RET * DK_RET
_VR0 = _KR0 + H_RET * DK_RET
_GR0 = _VR0 + W_RET
_QF0 = _GR0 + W_RET
_KF0 = _QF0 + W_FOX
_VF0 = _KF0 + W_FOX
_FL0 = _VF0 + W_FOX
_GF0 = _FL0 + H_FOX
_GA0 = _GF0 + W_FOX


def _cparams(*sem):
    return pltpu.CompilerParams(dimension_semantics=sem, vmem_limit_bytes=VMEM_LIMIT)


def _sigmoid(x):
    return 1.0 / (1.0 + jnp.exp(-x))


def _silu(x):
    return x * _sigmoid(x)


def _fill_normed(x_ref, g_ref, xn_ref):
    @pl.when(pl.program_id(1) == 0)
    def _():
        x = x_ref[...]
        ms = jnp.mean(x * x, axis=-1, keepdims=True)
        xn_ref[...] = (x * lax.rsqrt(ms + EPS) * g_ref[...]).astype(BF16)


def _proj_plain_kernel(x_ref, g_ref, w_ref, *rest, n_out, stacked):
    outs, xn_ref = rest[-n_out - 1:-1], rest[-1]
    _fill_normed(x_ref, g_ref, xn_ref)
    z = jnp.dot(xn_ref[...], w_ref[...], preferred_element_type=F32)
    for o, st in zip(outs, stacked):
        if st:
            o[0] = z.astype(o.dtype)
        else:
            o[...] = z.astype(o.dtype)


def _proj_rotary_kernel(x_ref, g_ref, w_ref, cos_ref, sin_ref, o_ref, xn_ref):
    _fill_normed(x_ref, g_ref, xn_ref)
    z = jnp.dot(xn_ref[...], w_ref[...], preferred_element_type=F32)
    cos = cos_ref[0]
    sin = sin_ref[0]
    for hh in range(z.shape[1] // LANES):
        zh = z[:, hh * LANES:(hh + 1) * LANES]
        rot = pltpu.roll(zh, LANES // 2, axis=1)
        o_ref[:, hh * LANES:(hh + 1) * LANES] = (zh * cos + rot * sin).astype(o_ref.dtype)


def _proj_headnorm_kernel(x_ref, g_ref, w_ref, hg_ref, *rest, n_out, stacked, scale):
    outs, xn_ref = rest[-n_out - 1:-1], rest[-1]
    _fill_normed(x_ref, g_ref, xn_ref)
    z = jnp.dot(xn_ref[...], w_ref[...], preferred_element_type=F32)
    hg = hg_ref[...] * scale
    for hh in range(z.shape[1] // LANES):
        zh = z[:, hh * LANES:(hh + 1) * LANES]
        ms = jnp.mean(zh * zh, axis=-1, keepdims=True)
        y = zh * lax.rsqrt(ms + EPS) * hg
        for o, st in zip(outs, stacked):
            if st:
                o[0, :, hh * LANES:(hh + 1) * LANES] = y.astype(o.dtype)
            else:
                o[:, hh * LANES:(hh + 1) * LANES] = y.astype(o.dtype)


def _proj_logsig_kernel(x_ref, g_ref, w_ref, b_ref, o_ref, xn_ref):
    _fill_normed(x_ref, g_ref, xn_ref)
    z = jnp.dot(xn_ref[...], w_ref[...], preferred_element_type=F32) + b_ref[...]
    o_ref[...] = jnp.minimum(z, 0.0) - jnp.log1p(jnp.exp(-jnp.abs(z)))


def _proj_call(kern, x2d, g, w, extra_in, extra_specs, outs, *, tm, tn, layer=0, prev=()):
    m, d = x2d.shape
    ncol = w.shape[1]
    grid = (m // tm, ncol // tn)
    in_specs = [
        pl.BlockSpec((tm, d), lambda i, j: (i, 0)),
        pl.BlockSpec((1, d), lambda i, j: (0, 0)),
        pl.BlockSpec((d, tn), lambda i, j: (0, j)),
    ] + list(extra_specs)
    args = [x2d, g.reshape(1, d), w] + list(extra_in)
    out_shapes, out_specs = [], []
    for cols, dt, depth in outs:
        if depth is None:
            out_shapes.append(jax.ShapeDtypeStruct((m, cols), dt))
            out_specs.append(pl.BlockSpec((tm, tn), lambda i, j: (i, j)))
        else:
            out_shapes.append(jax.ShapeDtypeStruct((depth, m, cols), dt))
            out_specs.append(pl.BlockSpec((1, tm, tn), lambda i, j, _l=layer: (_l, i, j)))
    aliases = {}
    if prev:
        stacked_idx = [k for k, (_, _, depth) in enumerate(outs) if depth is not None]
        for buf, k in zip(prev, stacked_idx):
            aliases[len(args)] = k
            args.append(buf)
            in_specs.append(pl.BlockSpec(memory_space=pl.ANY))
    base = getattr(kern, "func", kern).__name__.strip("_").replace("_kernel", "")
    return pl.pallas_call(
        kern,
        name=f"{base}_m{m}",
        grid=grid,
        in_specs=in_specs,
        out_specs=out_specs,
        out_shape=out_shapes,
        scratch_shapes=[pltpu.VMEM((tm, d), BF16)],
        input_output_aliases=aliases,
        compiler_params=_cparams("arbitrary", "arbitrary"),
    )(*args)


def _split3(x):
    hi = x.astype(BF16)
    r1 = x - hi.astype(F32)
    mid = r1.astype(BF16)
    lo = (r1 - mid.astype(F32)).astype(BF16)
    return hi, mid, lo


def _cumsum_kernel(x_ref, tri_ref, o_ref):
    c = tri_ref.shape[0]
    n = x_ref.shape[1] // c
    tri = tri_ref[...]

    def body(k, carry):
        off = pl.multiple_of(k * c, c)
        x = x_ref[0, pl.ds(off, c), :]
        hi, mid, lo = _split3(x)
        f = (jnp.dot(tri, hi, preferred_element_type=F32)
             + jnp.dot(tri, mid, preferred_element_type=F32)
             + jnp.dot(tri, lo, preferred_element_type=F32)) + carry
        o_ref[0, pl.ds(off, c), :] = f
        return f[c - 1:c, :]

    lax.fori_loop(0, n, body, jnp.zeros((1, x_ref.shape[2]), F32))


def _cumsum_rows(x3d, chunk=128):
    b, s, w = x3d.shape
    tri = jnp.asarray(np.tril(np.ones((chunk, chunk), np.float32)), BF16)
    return pl.pallas_call(
        _cumsum_kernel,
        name="cumsum_rows",
        grid=(b,),
        in_specs=[pl.BlockSpec((1, s, w), lambda i: (i, 0, 0)),
                  pl.BlockSpec((chunk, chunk), lambda i: (0, 0))],
        out_specs=pl.BlockSpec((1, s, w), lambda i: (i, 0, 0)),
        out_shape=jax.ShapeDtypeStruct((b, s, w), F32),
        compiler_params=_cparams("arbitrary"),
    )(x3d, tri)


def _groupnorm_gate(o, gain, gate):
    mu = jnp.mean(o, axis=-1, keepdims=True)
    d = o - mu
    var = jnp.mean(d * d, axis=-1, keepdims=True)
    return d * lax.rsqrt(var + EPS) * gain * _silu(gate)


def _ret_prompt_kernel(qk_ref, v_ref, gr_ref, dmat_ref, qdec_ref, kdec_ref, sdec_ref, gn_ref,
                       a_ref, sout_ref, s_ref):
    r = pl.program_id(1)

    @pl.when(r == 0)
    def _():
        s_ref[...] = jnp.zeros_like(s_ref)

    rb = qk_ref.shape[0]
    for c in range(rb // RET_CHUNK):
        rows = slice(c * RET_CHUNK, (c + 1) * RET_CHUNK)
        for h in range(H_RET):
            q = qk_ref[rows, h * DK_RET:(h + 1) * DK_RET]
            k = qk_ref[rows, (H_RET + h) * DK_RET:(H_RET + h + 1) * DK_RET]
            v = v_ref[rows, h * DV_RET:(h + 1) * DV_RET]
            s_prev = s_ref[h]
            sc = lax.dot_general(q, k, (((1,), (1,)), ((), ())), preferred_element_type=F32) * dmat_ref[h]
            inner = jnp.dot(sc.astype(BF16), v, preferred_element_type=F32)
            cross = jnp.dot(q, s_prev.astype(BF16), preferred_element_type=F32) * qdec_ref[h]
            o = inner + cross
            kd = (k.astype(F32) * kdec_ref[h]).astype(BF16)
            s_ref[h] = sdec_ref[h] * s_prev + lax.dot_general(
                kd, v, (((0,), (0,)), ((), ())), preferred_element_type=F32)
            cols = slice(h * DV_RET, (h + 1) * DV_RET)
            a_ref[rows, cols] = _groupnorm_gate(
                o, gn_ref[:, cols], gr_ref[rows, cols].astype(F32)).astype(a_ref.dtype)

    @pl.when(r == pl.num_programs(1) - 1)
    def _():
        sout_ref[0] = s_ref[...]


def _ret_tables():
    lg = np.log1p(-np.exp2(-5.0 - np.arange(H_RET, dtype=np.float64)))
    idx = np.arange(RET_CHUNK, dtype=np.float64)
    rel = idx[:, None] - idx[None, :]
    dmat = np.where(rel >= 0, np.exp(lg[:, None, None] * np.maximum(rel, 0.0)[None]), 0.0)
    qdec = np.exp(lg[:, None] * (idx[None, :] + 1.0))
    kdec = np.exp(lg[:, None] * (RET_CHUNK - 1.0 - idx)[None, :])
    sdec = np.exp(lg * RET_CHUNK)
    gamma = np.exp(lg)
    return (jnp.asarray(dmat, F32),
            jnp.asarray(np.broadcast_to(qdec[:, :, None], (H_RET, RET_CHUNK, DV_RET)), F32),
            jnp.asarray(np.broadcast_to(kdec[:, :, None], (H_RET, RET_CHUNK, DK_RET)), F32),
            jnp.asarray(np.broadcast_to(sdec[:, None, None], (H_RET, 1, DV_RET)), F32),
            [float(x) for x in gamma])


def _ret_prompt(qk, vg, gn_g, batch, seq, rb):
    n = qk.shape[0]
    nrb = seq // rb
    dmat, qdec, kdec, sdec, _ = _ret_tables()
    full3 = lambda b, r: (0, 0, 0)
    return pl.pallas_call(
        _ret_prompt_kernel,
        name="ret_prompt",
        grid=(batch, nrb),
        in_specs=[
            pl.BlockSpec((rb, 2 * H_RET * DK_RET), lambda b, r: (b * nrb + r, 0)),
            pl.BlockSpec((rb, W_RET), lambda b, r: (b * nrb + r, 0)),
            pl.BlockSpec((rb, W_RET), lambda b, r: (b * nrb + r, 1)),
            pl.BlockSpec(dmat.shape, full3),
            pl.BlockSpec(qdec.shape, full3),
            pl.BlockSpec(kdec.shape, full3),
            pl.BlockSpec(sdec.shape, full3),
            pl.BlockSpec((1, W_RET), lambda b, r: (0, 0)),
        ],
        out_specs=[
            pl.BlockSpec((rb, W_RET), lambda b, r: (b * nrb + r, 0)),
            pl.BlockSpec((1, H_RET, DK_RET, DV_RET), lambda b, r: (b, 0, 0, 0)),
        ],
        out_shape=[
            jax.ShapeDtypeStruct((n, W_RET), BF16),
            jax.ShapeDtypeStruct((batch, H_RET, DK_RET, DV_RET), F32),
        ],
        scratch_shapes=[pltpu.VMEM((H_RET, DK_RET, DV_RET), F32)],
        compiler_params=_cparams("arbitrary", "arbitrary"),
    )(qk, vg, vg, dmat, qdec, kdec, sdec, gn_g.reshape(1, W_RET))


def _fox_prompt_kernel(q_ref, k_ref, v_ref, f_ref, gf_ref, o_ref, *, tk):
    i = pl.program_id(2)
    tq = q_ref.shape[0]
    q = q_ref[...]
    nsub = tq // tk

    def step(off, carry, masked_chunk):
        m, l, acc = carry
        k = k_ref[pl.ds(off, tk), :]
        v = v_ref[pl.ds(off, tk), :]
        s = lax.dot_general(q, k, (((1,), (1,)), ((), ())), preferred_element_type=F32)
        s = s - f_ref[0, :, pl.ds(off, tk)]
        if masked_chunk is not None:
            row = lax.broadcasted_iota(jnp.int32, (tq, tk), 0)
            col = lax.broadcasted_iota(jnp.int32, (tq, tk), 1) + masked_chunk * tk
            s = jnp.where(col <= row, s, NEG_BIG)
        m_new = jnp.maximum(m, jnp.max(s, axis=-1, keepdims=True))
        alpha = jnp.exp(m - m_new)
        p = jnp.exp(s - m_new)
        l = alpha * l + jnp.sum(p, axis=-1, keepdims=True)
        acc = alpha * acc + jnp.dot(p.astype(BF16), v, preferred_element_type=F32)
        return m_new, l, acc

    init = (jnp.full((tq, 1), NEG_BIG, F32), jnp.zeros((tq, 1), F32), jnp.zeros((tq, HD_FOX), F32))
    carry = lax.fori_loop(
        0, i * nsub, lambda c, cr: step(pl.multiple_of(c * tk, tk), cr, None), init)
    for cc in range(nsub):
        off = pl.multiple_of((i * nsub + cc) * tk, tk)
        carry = step(off, carry, cc)
    m, l, acc = carry
    o_ref[...] = (acc / l * _silu(gf_ref[...].astype(F32))).astype(o_ref.dtype)


def _fox_prompt(q, k, v, frow, gates, batch, seq, tq, tk):
    n = q.shape[0]
    nq = seq // tq
    return pl.pallas_call(
        functools.partial(_fox_prompt_kernel, tk=tk),
        name="fox_prompt",
        grid=(batch, H_FOX, nq),
        in_specs=[
            pl.BlockSpec((tq, HD_FOX), lambda b, h, i: (b * nq + i, h)),
            pl.BlockSpec((seq, HD_FOX), lambda b, h, i: (b, h)),
            pl.BlockSpec((seq, HD_FOX), lambda b, h, i: (b, h)),
            pl.BlockSpec((1, 1, seq), lambda b, h, i: (b * H_FOX + h, 0, 0)),
            pl.BlockSpec((tq, HD_FOX), lambda b, h, i: (b * nq + i, h)),
        ],
        out_specs=pl.BlockSpec((tq, HD_FOX), lambda b, h, i: (b * nq + i, h)),
        out_shape=jax.ShapeDtypeStruct((n, W_FOX), BF16),
        compiler_params=_cparams("arbitrary", "arbitrary", "arbitrary"),
    )(q, k, v, frow, gates)


def _combine_kernel(a_ref, b_ref, ga_ref, gb_ref, h_ref, p_ref, wra_ref, wfa_ref, wout_ref,
                    wpg_ref, wpp_ref, o_ref):
    a = jnp.dot(a_ref[...].astype(BF16), wra_ref[...], preferred_element_type=F32)
    b = jnp.dot(b_ref[...].astype(BF16), wfa_ref[...], preferred_element_type=F32)
    m = _sigmoid(ga_ref[...].astype(F32)) * a + _sigmoid(gb_ref[...].astype(F32)) * b
    h1 = h_ref[...] + jnp.dot(m.astype(BF16), wout_ref[...], preferred_element_type=F32)
    gate = _sigmoid(jnp.dot(h1.astype(BF16), wpg_ref[...], preferred_element_type=F32))
    pe = jnp.dot(p_ref[0].astype(BF16), wpp_ref[...], preferred_element_type=F32)
    o_ref[...] = h1 + gate * pe


def _combine(a_in, b_in, gates, h2d, p3d, layer, wra, wfa, wout, wpg, wpp, tm):
    n, d = h2d.shape
    dp = p3d.shape[-1]
    row = lambda i: (i, 0)
    const = lambda i: (0, 0)
    wspec = lambda w: pl.BlockSpec(w.shape, const, pipeline_mode=pl.Buffered(1))
    return pl.pallas_call(
        _combine_kernel,
        name=f"combine_m{n}",
        grid=(n // tm,),
        in_specs=[
            pl.BlockSpec((tm, W_RET), row),
            pl.BlockSpec((tm, W_FOX), row),
            pl.BlockSpec((tm, d), lambda i: (i, 1)),
            pl.BlockSpec((tm, d), lambda i: (i, 2)),
            pl.BlockSpec((tm, d), row),
            pl.BlockSpec((1, tm, dp), lambda i, _l=layer: (_l, i, 0)),
            wspec(wra), wspec(wfa), wspec(wout), wspec(wpg), wspec(wpp),
        ],
        out_specs=pl.BlockSpec((tm, d), row),
        out_shape=jax.ShapeDtypeStruct((n, d), F32),
        compiler_params=_cparams("arbitrary"),
    )(a_in, b_in, gates, gates, h2d, p3d, wra, wfa, wout, wpg, wpp)


def _ret_sample_kernel(qk_ref, v_ref, gr_ref, st_ref, gn_ref, a_ref, sout_ref, *, gamma):
    qk = qk_ref[0]
    vv = v_ref[0]
    gg = gr_ref[0]
    row8 = lax.broadcasted_iota(jnp.int32, (8, 1), 0)
    for h in range(H_RET):
        q = qk[:, h * DK_RET:(h + 1) * DK_RET]
        k = qk[:, (H_RET + h) * DK_RET:(H_RET + h + 1) * DK_RET]
        cols = slice(h * DV_RET, (h + 1) * DV_RET)
        v = vv[:, cols]
        s_prev = st_ref[0, 0, h]
        q8 = jnp.broadcast_to(q, (8, DK_RET)).astype(BF16)
        cross = jnp.dot(q8, s_prev.astype(BF16), preferred_element_type=F32)[0:1] * gamma[h]
        qk_dot = jnp.sum(q * k, axis=-1, keepdims=True)
        o = qk_dot * v + cross
        k8 = jnp.where(row8 == 0, jnp.broadcast_to(k, (8, DK_RET)), 0.0).astype(BF16)
        v8 = jnp.broadcast_to(v, (8, DV_RET)).astype(BF16)
        outer = lax.dot_general(k8, v8, (((0,), (0,)), ((), ())), preferred_element_type=F32)
        sout_ref[0, h] = gamma[h] * s_prev + outer
        a_ref[0, :, cols] = _groupnorm_gate(o, gn_ref[:, cols], gg[:, cols])


def _ret_sample(qk3, vg3, state, layer, gn_g):
    nb = qk3.shape[0]
    gamma = _ret_tables()[4]
    return pl.pallas_call(
        functools.partial(_ret_sample_kernel, gamma=gamma),
        name="ret_sample",
        grid=(nb,),
        in_specs=[
            pl.BlockSpec((1, 1, 2 * H_RET * DK_RET), lambda b: (b, 0, 0)),
            pl.BlockSpec((1, 1, W_RET), lambda b: (b, 0, 0)),
            pl.BlockSpec((1, 1, W_RET), lambda b: (b, 0, 1)),
            pl.BlockSpec((1, 1, H_RET, DK_RET, DV_RET), lambda b, _l=layer: (_l, b, 0, 0, 0)),
            pl.BlockSpec((1, W_RET), lambda b: (0, 0)),
        ],
        out_specs=[
            pl.BlockSpec((1, 1, W_RET), lambda b: (b, 0, 0)),
            pl.BlockSpec((1, H_RET, DK_RET, DV_RET), lambda b: (b, 0, 0, 0)),
        ],
        out_shape=[
            jax.ShapeDtypeStruct((nb, 1, W_RET), F32),
            jax.ShapeDtypeStruct((nb, H_RET, DK_RET, DV_RET), F32),
        ],
        compiler_params=_cparams("arbitrary"),
    )(qk3, vg3, vg3, state, gn_g.reshape(1, W_RET))


def _fox_sample_kernel(pt_ref, q_ref, kn_ref, vn_ref, fn_ref, gf_ref, ck_ref, cv_ref, cf_ref, u_ref,
                       o_ref, qbd_ref, m_ref, l_ref, acc_ref, c_ref):
    j = pl.program_id(1)
    hrow = lax.broadcasted_iota(jnp.int32, (H_FOX, W_FOX), 0)
    hcol = lax.broadcasted_iota(jnp.int32, (H_FOX, W_FOX), 1) // HD_FOX
    diag = hrow == hcol

    @pl.when(j == 0)
    def _():
        qbd_ref[...] = jnp.where(diag, jnp.broadcast_to(q_ref[0], (H_FOX, W_FOX)), 0.0)
        m_ref[...] = jnp.full_like(m_ref, NEG_BIG)
        l_ref[...] = jnp.zeros_like(l_ref)
        acc_ref[...] = jnp.zeros_like(acc_ref)
        c_ref[...] = jnp.zeros_like(c_ref)

    lf = cf_ref[0, 0]
    hi, mid, lo = _split3(lf)
    u = u_ref[...]
    suf = (jnp.dot(hi, u, preferred_element_type=F32) + jnp.dot(mid, u, preferred_element_type=F32)
           + jnp.dot(lo, u, preferred_element_type=F32))
    bias = suf + (c_ref[...] + fn_ref[0])
    kp = ck_ref[0, 0].astype(BF16)
    s = lax.dot_general(qbd_ref[...].astype(BF16), kp, (((1,), (1,)), ((), ())),
                        preferred_element_type=F32) + bias
    m_new = jnp.maximum(m_ref[...], jnp.max(s, axis=-1, keepdims=True))
    alpha = jnp.exp(m_ref[...] - m_new)
    p = jnp.exp(s - m_new)
    l_ref[...] = alpha * l_ref[...] + jnp.sum(p, axis=-1, keepdims=True)
    acc_ref[...] = alpha * acc_ref[...] + jnp.dot(p.astype(BF16), cv_ref[0, 0].astype(BF16),
                                                  preferred_element_type=F32)
    m_ref[...] = m_new
    c_ref[...] = c_ref[...] + jnp.sum(lf, axis=-1, keepdims=True)

    @pl.when(j == pl.num_programs(1) - 1)
    def _():
        s_new = jnp.sum(qbd_ref[...] * kn_ref[0], axis=-1, keepdims=True)
        m_f = jnp.maximum(m_ref[...], s_new)
        al = jnp.exp(m_ref[...] - m_f)
        p_new = jnp.exp(s_new - m_f)
        l_f = al * l_ref[...] + p_new
        full = (al * acc_ref[...] + p_new * vn_ref[0]) / l_f
        out = jnp.sum(jnp.where(diag, full, 0.0), axis=0, keepdims=True)
        o_ref[0] = out * _silu(gf_ref[0])


def _fox_sample(q3, kn3, vn3, fn_col, gates3, cache_k4, cache_v4, cache_ft, page_table, layer):
    nb, n_pages = page_table.shape
    page = cache_k4.shape[2]
    u = jnp.asarray(np.tril(np.ones((page, page), np.float32), -1), BF16)
    last = n_pages - 1
    vec = lambda b, j, pt: (b, 0, 0)
    pg = lambda b, j, pt, _l=layer: (_l, pt[b, last - j], 0, 0)
    grid_spec = pltpu.PrefetchScalarGridSpec(
        num_scalar_prefetch=1,
        grid=(nb, n_pages),
        in_specs=[
            pl.BlockSpec((1, 1, W_FOX), vec),
            pl.BlockSpec((1, 1, W_FOX), vec),
            pl.BlockSpec((1, 1, W_FOX), vec),
            pl.BlockSpec((1, H_FOX, 1), vec),
            pl.BlockSpec((1, 1, W_FOX), vec),
            pl.BlockSpec((1, 1, page, W_FOX), pg),
            pl.BlockSpec((1, 1, page, W_FOX), pg),
            pl.BlockSpec((1, 1, H_FOX, page), pg),
            pl.BlockSpec((page, page), lambda b, j, pt: (0, 0)),
        ],
        out_specs=pl.BlockSpec((1, 1, W_FOX), vec),
        scratch_shapes=[
            pltpu.VMEM((H_FOX, W_FOX), F32),
            pltpu.VMEM((H_FOX, 1), F32),
            pltpu.VMEM((H_FOX, 1), F32),
            pltpu.VMEM((H_FOX, W_FOX), F32),
            pltpu.VMEM((H_FOX, 1), F32),
        ],
    )
    return pl.pallas_call(
        _fox_sample_kernel,
        name="fox_sample",
        grid_spec=grid_spec,
        out_shape=jax.ShapeDtypeStruct((nb, 1, W_FOX), F32),
        compiler_params=_cparams("arbitrary", "arbitrary"),
    )(page_table, q3, kn3, vn3, fn_col, gates3, cache_k4, cache_v4, cache_ft, u)


def _rotary_tables(pos, k_scale):
    half = DK_RET // 2
    inv = ROPE_BASE ** (-np.arange(half, dtype=np.float64) / half)
    ang = np.asarray(pos, np.float64)[:, None] * inv[None, :]
    cos = np.concatenate([np.cos(ang), np.cos(ang)], axis=-1)
    sin = np.concatenate([-np.sin(ang), np.sin(ang)], axis=-1)
    return (jnp.asarray(np.stack([cos, cos * k_scale]), F32),
            jnp.asarray(np.stack([sin, sin * k_scale]), F32))


def _project_group(x2d, lw, rot, *, tm, layer, depth, stacked_prev, act_dtype):
    m, d = x2d.shape
    cos, sin = rot
    nt = cos.shape[1] // tm
    g = lw["norm_g"]
    hw = H_RET * DK_RET
    out = {}
    tab_spec = pl.BlockSpec((1, tm, LANES), lambda i, j: (j, i % nt, 0))
    out["qk_r"] = _proj_call(_proj_rotary_kernel, x2d, g, lw["w_qk"], [cos, sin], [tab_spec, tab_spec],
                             [(2 * hw, act_dtype, None)], tm=tm, tn=hw)[0]
    out["vg_r"] = _proj_call(functools.partial(_proj_plain_kernel, n_out=1, stacked=(False,)),
                             x2d, g, lw["w_vg"], [], [], [(2 * W_RET, act_dtype, None)], tm=tm, tn=512)[0]
    out["gates"] = _proj_call(functools.partial(_proj_plain_kernel, n_out=1, stacked=(False,)),
                              x2d, g, lw["w_gates"], [], [], [(3 * d, act_dtype, None)], tm=tm, tn=512)[0]
    hg_spec = pl.BlockSpec((1, HD_FOX), lambda i, j: (0, 0))
    out["q_f"] = _proj_call(
        functools.partial(_proj_headnorm_kernel, n_out=1, stacked=(False,), scale=HD_FOX ** -0.5),
        x2d, g, lw["w_qf"], [lw["q_norm_g"].reshape(1, HD_FOX)], [hg_spec],
        [(W_FOX, act_dtype, None)], tm=tm, tn=512)[0]
    if depth is None:
        out["k_f"] = _proj_call(
            functools.partial(_proj_headnorm_kernel, n_out=1, stacked=(False,), scale=1.0),
            x2d, g, lw["w_kf"], [lw["k_norm_g"].reshape(1, HD_FOX)], [hg_spec],
            [(W_FOX, F32, None)], tm=tm, tn=512)[0]
        out["v_f"] = _proj_call(functools.partial(_proj_plain_kernel, n_out=1, stacked=(False,)),
                                x2d, g, lw["w_vf"], [], [], [(W_FOX, F32, None)], tm=tm, tn=512)[0]
    else:
        prev_k, prev_v = stacked_prev
        out["k_stack"], out["k_f"] = _proj_call(
            functools.partial(_proj_headnorm_kernel, n_out=2, stacked=(True, False), scale=1.0),
            x2d, g, lw["w_kf"], [lw["k_norm_g"].reshape(1, HD_FOX)], [hg_spec],
            [(W_FOX, F32, depth), (W_FOX, BF16, None)], tm=tm, tn=512, layer=layer,
            prev=() if prev_k is None else (prev_k,))
        out["v_stack"], out["v_f"] = _proj_call(
            functools.partial(_proj_plain_kernel, n_out=2, stacked=(True, False)),
            x2d, g, lw["w_vf"], [], [], [(W_FOX, F32, depth), (W_FOX, BF16, None)], tm=tm, tn=512,
            layer=layer, prev=() if prev_v is None else (prev_v,))
    b_spec = pl.BlockSpec((1, LANES), lambda i, j: (0, 0))
    out["logf"] = _proj_call(_proj_logsig_kernel, x2d, g, lw["w_fl"], [lw["b_fl"]], [b_spec],
                             [(LANES, F32, None)], tm=tm, tn=LANES)[0]
    return out


def _layer_weights(l, norm_g, w_in, b_forget, q_norm_g, k_norm_g, ret_gn_g, w_branch_ret, w_branch_fox,
                   w_out, w_ple_gate, w_ple_proj):
    w = w_in[l]
    d = w.shape[0]
    bf = lambda a: a.astype(BF16)
    pad = LANES - H_FOX
    return {
        "norm_g": norm_g[l],
        "w_qk": bf(w[:, _QR0:_VR0]),
        "w_vg": bf(w[:, _VR0:_QF0]),
        "w_qf": bf(w[:, _QF0:_KF0]),
        "w_kf": bf(w[:, _KF0:_VF0]),
        "w_vf": bf(w[:, _VF0:_FL0]),
        "w_fl": bf(jnp.pad(w[:, _FL0:_GF0], ((0, 0), (0, pad)))),
        "b_fl": jnp.pad(b_forget[l], (0, pad)).reshape(1, LANES).astype(F32),
        "w_gates": bf(w[:, _GF0:]),
        "q_norm_g": q_norm_g[l],
        "k_norm_g": k_norm_g[l],
        "ret_gn_g": ret_gn_g[l],
        "wra": bf(w_branch_ret[l]),
        "wfa": bf(w_branch_fox[l]),
        "wout": bf(w_out[l]),
        "wpg": bf(w_ple_gate[l]),
        "wpp": bf(w_ple_proj[l]),
    }


def kernel(x_prompt, x_sample, cache_k, cache_v, cache_logf, state_ret, page_table, p_prompt, p_sample,
           norm_g, w_in, b_forget, q_norm_g, k_norm_g, ret_gn_g, w_branch_ret, w_branch_fox, w_out,
           w_ple_gate, w_ple_proj):
    batch, seq, d = x_prompt.shape
    nb, tdec, _ = x_sample.shape
    depth, n_pool, page = cache_k.shape[0], cache_k.shape[1], cache_k.shape[2]
    past_len = page_table.shape[1] * page
    n = batch * seq
    ns = nb * tdec
    assert tdec == 1, "decode group handles one new position per sequence"

    tm_p = min(1024, seq)
    rb = min(512, seq)
    tq = min(512, seq)
    tm_c = min(512, seq)

    rot_p = _rotary_tables(np.arange(seq), DK_RET ** -0.5)
    rot_s = _rotary_tables(np.tile(past_len + np.arange(tdec), nb), DK_RET ** -0.5)

    ck4 = cache_k.reshape(depth, n_pool, page, W_FOX)
    cv4 = cache_v.reshape(depth, n_pool, page, W_FOX)
    cft = jnp.swapaxes(cache_logf, 2, 3)
    p_p = p_prompt.reshape(depth, n, p_prompt.shape[-1])
    p_s = p_sample.reshape(depth, ns, p_sample.shape[-1])

    hp = x_prompt.reshape(n, d)
    hs = x_sample.reshape(ns, d)
    k_stack = v_stack = None
    fp_l, sp_l, ks_l, vs_l, fs_l, ss_l = [], [], [], [], [], []
    for l in range(depth):
        lw = _layer_weights(l, norm_g, w_in, b_forget, q_norm_g, k_norm_g, ret_gn_g, w_branch_ret,
                            w_branch_fox, w_out, w_ple_gate, w_ple_proj)
        pr = _project_group(hp, lw, rot_p, tm=tm_p, layer=l, depth=depth,
                            stacked_prev=(k_stack, v_stack), act_dtype=BF16)
        k_stack, v_stack = pr["k_stack"], pr["v_stack"]
        logf = pr["logf"]
        fp_l.append(logf[:, :H_FOX])
        fcum = _cumsum_rows(logf.reshape(batch, seq, LANES))
        frow = jnp.swapaxes(fcum[:, :, :H_FOX], 1, 2).reshape(batch * H_FOX, 1, seq)
        a_in, s_fin = _ret_prompt(pr["qk_r"], pr["vg_r"], lw["ret_gn_g"], batch, seq, rb)
        sp_l.append(s_fin)
        b_in = _fox_prompt(pr["q_f"], pr["k_f"], pr["v_f"], frow, pr["gates"], batch, seq, tq, tq)
        hp = _combine(a_in, b_in, pr["gates"], hp, p_p, l, lw["wra"], lw["wfa"], lw["wout"],
                      lw["wpg"], lw["wpp"], tm_c)
        sr = _project_group(hs, lw, rot_s, tm=ns, layer=l, depth=None, stacked_prev=None, act_dtype=F32)
        ks_l.append(sr["k_f"])
        vs_l.append(sr["v_f"])
        logf_s = sr["logf"][:, :H_FOX]
        fs_l.append(logf_s)
        a_s, s_new = _ret_sample(sr["qk_r"].reshape(ns, 1, -1), sr["vg_r"].reshape(ns, 1, -1),
                                 state_ret, l, lw["ret_gn_g"])
        ss_l.append(s_new)
        gates_s = sr["gates"].reshape(ns, 1, -1)
        b_s = _fox_sample(sr["q_f"].reshape(ns, 1, -1), sr["k_f"].reshape(ns, 1, -1),
                          sr["v_f"].reshape(ns, 1, -1), logf_s.reshape(ns, H_FOX, 1), gates_s,
                          ck4, cv4, cft, page_table, l)
        hs = _combine(a_s.reshape(ns, -1), b_s.reshape(ns, -1), sr["gates"], hs, p_s, l, lw["wra"],
                      lw["wfa"], lw["wout"], lw["wpg"], lw["wpp"], ns)

    return (
        hp.reshape(batch, seq, d),
        hs.reshape(nb, tdec, d),
        k_stack.reshape(depth, batch, seq, H_FOX, HD_FOX),
        v_stack.reshape(depth, batch, seq, H_FOX, HD_FOX),
        jnp.stack(fp_l).reshape(depth, batch, seq, H_FOX),
        jnp.stack(sp_l),
        jnp.stack(ks_l).reshape(depth, nb, tdec, H_FOX, HD_FOX),
        jnp.stack(vs_l).reshape(depth, nb, tdec, H_FOX, HD_FOX),
        jnp.stack(fs_l).reshape(depth, nb, tdec, H_FOX),
        jnp.stack(ss_l),
    )
```

```python
import functools
import math

import numpy as np
import jax
import jax.numpy as jnp
from jax import lax
from jax.experimental import pallas as pl
from jax.experimental.pallas import tpu as pltpu

F32 = jnp.float32
BF16 = jnp.bfloat16

H_RET = 4
DK_RET = 128
DV_RET = 256
RET_CHUNK = 128
W_RET = H_RET * DV_RET
H_FOX = 8
HD_FOX = 128
W_FOX = H_FOX * HD_FOX
ROPE_BASE = 10000.0
EPS = 1e-6
LOG2E = math.log2(math.e)
LANES = 128
NEG_BIG = -1e30
VMEM_LIMIT = 56 * 1024 * 1024
PROJ_COLS = 512

_QR0 = 0
_KR0 = _QR0 + H_RET * DK_RET
_VR0 = _KR0 + H_RET * DK_RET
_GR0 = _VR0 + W_RET
_QF0 = _GR0 + W_RET
_KF0 = _QF0 + W_FOX
_VF0 = _KF0 + W_FOX
_FL0 = _VF0 + W_FOX
_GF0 = _FL0 + H_FOX
_GA0 = _GF0 + W_FOX


def _cparams(*sem):
    return pltpu.CompilerParams(dimension_semantics=sem, vmem_limit_bytes=VMEM_LIMIT)


def _sigmoid(x):
    return 1.0 / (1.0 + jnp.exp(-x))


def _silu(x):
    return x * _sigmoid(x)


def _split3(x):
    hi = x.astype(BF16)
    r1 = x - hi.astype(F32)
    mid = r1.astype(BF16)
    lo = (r1 - mid.astype(F32)).astype(BF16)
    return hi, mid, lo


def _proj_fused_kernel(x_ref, g_ref, cos_ref, sin_ref, hgq_ref, hgk_ref, bfl_ref,
                       wqk_ref, wvg_ref, wqf_ref, wkf_ref, wvf_ref, wfl_ref, wg_ref, *rest, n_alias):
    (oqk_ref, ovg_ref, oqf_ref, okst_ref, okb_ref, ovst_ref, ovb_ref, olf_ref, og_ref,
     xn_ref) = rest[n_alias:]
    x = x_ref[...]
    ms = jnp.mean(x * x, axis=-1, keepdims=True)
    xn_ref[...] = (x * lax.rsqrt(ms + EPS) * g_ref[...]).astype(BF16)

    def seg(w_ref, c0):
        return jnp.dot(xn_ref[...], w_ref[:, c0:c0 + PROJ_COLS], preferred_element_type=F32)

    heads = PROJ_COLS // LANES
    for c0 in range(0, wqk_ref.shape[1], PROJ_COLS):
        z = seg(wqk_ref, c0)
        which = 0 if c0 < H_RET * DK_RET else 1
        cos, sin = cos_ref[which], sin_ref[which]
        for hh in range(heads):
            zh = z[:, hh * LANES:(hh + 1) * LANES]
            rot = pltpu.roll(zh, LANES // 2, axis=1)
            oqk_ref[:, c0 + hh * LANES:c0 + (hh + 1) * LANES] = (zh * cos + rot * sin).astype(oqk_ref.dtype)
    for w_ref, o_ref in ((wvg_ref, ovg_ref), (wg_ref, og_ref)):
        for c0 in range(0, w_ref.shape[1], PROJ_COLS):
            o_ref[:, c0:c0 + PROJ_COLS] = seg(w_ref, c0).astype(o_ref.dtype)
    hgq = hgq_ref[...] * (HD_FOX ** -0.5 * LOG2E)
    hgk = hgk_ref[...]
    for c0 in range(0, W_FOX, PROJ_COLS):
        zq = seg(wqf_ref, c0)
        zk = seg(wkf_ref, c0)
        for hh in range(heads):
            cols = slice(c0 + hh * LANES, c0 + (hh + 1) * LANES)
            zh = zq[:, hh * LANES:(hh + 1) * LANES]
            yq = zh * lax.rsqrt(jnp.mean(zh * zh, axis=-1, keepdims=True) + EPS) * hgq
            oqf_ref[:, cols] = yq.astype(oqf_ref.dtype)
            zh = zk[:, hh * LANES:(hh + 1) * LANES]
            yk = zh * lax.rsqrt(jnp.mean(zh * zh, axis=-1, keepdims=True) + EPS) * hgk
            okst_ref[0, :, cols] = yk
            okb_ref[:, cols] = yk.astype(okb_ref.dtype)
    ones = jnp.ones((x.shape[0], LANES), ovb_ref.dtype)
    for c0 in range(0, W_FOX, PROJ_COLS):
        z = seg(wvf_ref, c0)
        ovst_ref[0, :, c0:c0 + PROJ_COLS] = z
        for hh in range(heads):
            h = c0 // LANES + hh
            ovb_ref[:, 2 * h * LANES:(2 * h + 1) * LANES] = z[:, hh * LANES:(hh + 1) * LANES].astype(ovb_ref.dtype)
            ovb_ref[:, (2 * h + 1) * LANES:(2 * h + 2) * LANES] = ones
    z = jnp.dot(xn_ref[...], wfl_ref[...], preferred_element_type=F32) + bfl_ref[...]
    olf_ref[...] = jnp.minimum(z, 0.0) - jnp.log1p(jnp.exp(-jnp.abs(z)))


def _proj_fused(x2d, lw, rot, *, tm, layer, depth, prev, act_dtype):
    m, d = x2d.shape
    cos, sin = rot
    nt = cos.shape[1] // tm
    row = lambda i: (i, 0)
    const = lambda i: (0, 0)
    tab = pl.BlockSpec((2, tm, LANES), lambda i: (0, i % nt, 0))
    resident = lambda a: pl.BlockSpec(a.shape, const, pipeline_mode=pl.Buffered(1))
    weights = [lw[k] for k in ("w_qk", "w_vg", "w_qf", "w_kf", "w_vf", "w_fl", "w_gates")]
    args = [x2d, lw["norm_g"].reshape(1, d), cos, sin, lw["q_norm_g"].reshape(1, HD_FOX),
            lw["k_norm_g"].reshape(1, HD_FOX), lw["b_fl"]] + weights
    in_specs = [pl.BlockSpec((tm, d), row), pl.BlockSpec((1, d), const), tab, tab,
                pl.BlockSpec((1, HD_FOX), const), pl.BlockSpec((1, HD_FOX), const),
                pl.BlockSpec((1, LANES), const)] + [resident(w) for w in weights]
    aliases = {}
    if prev is not None:
        aliases = {len(args): 3, len(args) + 1: 5}
        args += list(prev)
        in_specs += [pl.BlockSpec(memory_space=pl.ANY)] * 2
    flat = lambda cols, dt: (jax.ShapeDtypeStruct((m, cols), dt), pl.BlockSpec((tm, cols), row))
    stack = lambda cols: (jax.ShapeDtypeStruct((depth, m, cols), F32),
                          pl.BlockSpec((1, tm, cols), lambda i, _l=layer: (_l, i, 0)))
    outs = [flat(2 * H_RET * DK_RET, act_dtype), flat(2 * W_RET, act_dtype), flat(W_FOX, act_dtype),
            stack(W_FOX), flat(W_FOX, act_dtype), stack(W_FOX), flat(2 * W_FOX, act_dtype),
            flat(LANES, F32), flat(3 * d, act_dtype)]
    res = pl.pallas_call(
        functools.partial(_proj_fused_kernel, n_alias=len(aliases)),
        name=f"proj_fused_m{m}",
        grid=(m // tm,),
        in_specs=in_specs,
        out_specs=[o[1] for o in outs],
        out_shape=[o[0] for o in outs],
        scratch_shapes=[pltpu.VMEM((tm, d), BF16)],
        input_output_aliases=aliases,
        compiler_params=_cparams("arbitrary"),
    )(*args)
    names = ("qk_r", "vg_r", "q_f", "k_stack", "k_f", "v_stack", "v_aug", "logf", "gates")
    return dict(zip(names, res))


def _fbias_kernel(x_ref, tri_ref, sel_ref, o_ref):
    c = tri_ref.shape[0]
    n = x_ref.shape[1] // c
    tri = tri_ref[...]

    def body(k, carry):
        off = pl.multiple_of(k * c, c)
        hi, mid, lo = _split3(x_ref[0, pl.ds(off, c), :])
        f = (jnp.dot(tri, hi, preferred_element_type=F32)
             + jnp.dot(tri, mid, preferred_element_type=F32)
             + jnp.dot(tri, lo, preferred_element_type=F32)) + carry
        pieces = jnp.concatenate(_split3(f * (-LOG2E)), axis=1)
        for h in range(H_FOX):
            o_ref[0, pl.ds(off, c), h * LANES:(h + 1) * LANES] = jnp.dot(
                pieces, sel_ref[h], preferred_element_type=F32).astype(o_ref.dtype)
        return f[c - 1:c, :]

    lax.fori_loop(0, n, body, jnp.zeros((1, x_ref.shape[2]), F32))


def _fbias(logf3d, chunk=128):
    b, s, w = logf3d.shape
    tri = jnp.asarray(np.tril(np.ones((chunk, chunk), np.float32)), BF16)
    sel = np.zeros((H_FOX, 3 * LANES, LANES), np.float32)
    for h in range(H_FOX):
        for piece in range(3):
            sel[h, piece * LANES + h, piece] = 1.0
    sel = jnp.asarray(sel, BF16)
    return pl.pallas_call(
        _fbias_kernel,
        name="fbias",
        grid=(b,),
        in_specs=[pl.BlockSpec((1, s, w), lambda i: (i, 0, 0)),
                  pl.BlockSpec((chunk, chunk), lambda i: (0, 0)),
                  pl.BlockSpec(sel.shape, lambda i: (0, 0, 0))],
        out_specs=pl.BlockSpec((1, s, H_FOX * LANES), lambda i: (i, 0, 0)),
        out_shape=jax.ShapeDtypeStruct((b, s, H_FOX * LANES), BF16),
        compiler_params=_cparams("arbitrary"),
    )(logf3d, tri, sel)


def _groupnorm_gate(o, gain, gate):
    mu = jnp.mean(o, axis=-1, keepdims=True)
    d = o - mu
    var = jnp.mean(d * d, axis=-1, keepdims=True)
    return d * lax.rsqrt(var + EPS) * gain * _silu(gate)


def _ret_prompt_kernel(qk_ref, v_ref, gr_ref, dmat_ref, qdec_ref, kdec_ref, sdec_ref, gn_ref,
                       a_ref, sout_ref, s_ref):
    r = pl.program_id(1)

    @pl.when(r == 0)
    def _():
        s_ref[...] = jnp.zeros_like(s_ref)

    rb = qk_ref.shape[0]
    for c in range(rb // RET_CHUNK):
        rows = slice(c * RET_CHUNK, (c + 1) * RET_CHUNK)
        for h in range(H_RET):
            q = qk_ref[rows, h * DK_RET:(h + 1) * DK_RET]
            k = qk_ref[rows, (H_RET + h) * DK_RET:(H_RET + h + 1) * DK_RET]
            v = v_ref[rows, h * DV_RET:(h + 1) * DV_RET]
            s_prev = s_ref[h]
            sc = lax.dot_general(q, k, (((1,), (1,)), ((), ())), preferred_element_type=F32) * dmat_ref[h]
            inner = jnp.dot(sc.astype(BF16), v, preferred_element_type=F32)
            cross = jnp.dot(q, s_prev.astype(BF16), preferred_element_type=F32) * qdec_ref[h]
            o = inner + cross
            kd = (k.astype(F32) * kdec_ref[h]).astype(BF16)
            s_ref[h] = sdec_ref[h] * s_prev + lax.dot_general(
                kd, v, (((0,), (0,)), ((), ())), preferred_element_type=F32)
            cols = slice(h * DV_RET, (h + 1) * DV_RET)
            a_ref[rows, cols] = _groupnorm_gate(
                o, gn_ref[:, cols], gr_ref[rows, cols].astype(F32)).astype(a_ref.dtype)

    @pl.when(r == pl.num_programs(1) - 1)
    def _():
        sout_ref[0] = s_ref[...]


def _ret_tables():
    lg = np.log1p(-np.exp2(-5.0 - np.arange(H_RET, dtype=np.float64)))
    idx = np.arange(RET_CHUNK, dtype=np.float64)
    rel = idx[:, None] - idx[None, :]
    dmat = np.where(rel >= 0, np.exp(lg[:, None, None] * np.maximum(rel, 0.0)[None]), 0.0)
    qdec = np.exp(lg[:, None] * (idx[None, :] + 1.0))
    kdec = np.exp(lg[:, None] * (RET_CHUNK - 1.0 - idx)[None, :])
    sdec = np.exp(lg * RET_CHUNK)
    gamma = np.exp(lg)
    return (jnp.asarray(dmat, F32),
            jnp.asarray(np.broadcast_to(qdec[:, :, None], (H_RET, RET_CHUNK, DV_RET)), F32),
            jnp.asarray(np.broadcast_to(kdec[:, :, None], (H_RET, RET_CHUNK, DK_RET)), F32),
            jnp.asarray(np.broadcast_to(sdec[:, None, None], (H_RET, 1, DV_RET)), F32),
            [float(x) for x in gamma])


def _ret_prompt(qk, vg, gn_g, batch, seq, rb):
    n = qk.shape[0]
    nrb = seq // rb
    dmat, qdec, kdec, sdec, _ = _ret_tables()
    full3 = lambda b, r: (0, 0, 0)
    return pl.pallas_call(
        _ret_prompt_kernel,
        name="ret_prompt",
        grid=(batch, nrb),
        in_specs=[
            pl.BlockSpec((rb, 2 * H_RET * DK_RET), lambda b, r: (b * nrb + r, 0)),
            pl.BlockSpec((rb, W_RET), lambda b, r: (b * nrb + r, 0)),
            pl.BlockSpec((rb, W_RET), lambda b, r: (b * nrb + r, 1)),
            pl.BlockSpec(dmat.shape, full3),
            pl.BlockSpec(qdec.shape, full3),
            pl.BlockSpec(kdec.shape, full3),
            pl.BlockSpec(sdec.shape, full3),
            pl.BlockSpec((1, W_RET), lambda b, r: (0, 0)),
        ],
        out_specs=[
            pl.BlockSpec((rb, W_RET), lambda b, r: (b * nrb + r, 0)),
            pl.BlockSpec((1, H_RET, DK_RET, DV_RET), lambda b, r: (b, 0, 0, 0)),
        ],
        out_shape=[
            jax.ShapeDtypeStruct((n, W_RET), BF16),
            jax.ShapeDtypeStruct((batch, H_RET, DK_RET, DV_RET), F32),
        ],
        scratch_shapes=[pltpu.VMEM((H_RET, DK_RET, DV_RET), F32)],
        compiler_params=_cparams("arbitrary", "arbitrary"),
    )(qk, vg, vg, dmat, qdec, kdec, sdec, gn_g.reshape(1, W_RET))


def _fox_prompt_kernel(q_ref, k_ref, fb_ref, v_ref, gf_ref, o_ref, *, tk, nsplit):
    i = pl.program_id(2)
    tq = q_ref.shape[0]
    lane = lax.broadcasted_iota(jnp.int32, (tq, LANES), 1)
    q = jnp.concatenate([q_ref[...], jnp.where(lane < 3, 1.0, 0.0).astype(q_ref.dtype)], axis=1)
    nsub = tq // tk
    th = tq // nsplit

    def step(off, carry, diag_chunk):
        k = jnp.concatenate([k_ref[pl.ds(off, tk), :], fb_ref[0, pl.ds(off, tk), :]], axis=1)
        v = v_ref[pl.ds(off, tk), :]
        new = []
        for part, (m, acc) in enumerate(carry):
            row0 = part * th
            if diag_chunk is not None and diag_chunk * tk > row0 + th - 1:
                new.append((m, acc))
                continue
            s = lax.dot_general(q[row0:row0 + th], k, (((1,), (1,)), ((), ())),
                                preferred_element_type=F32)
            if diag_chunk is not None and diag_chunk * tk + tk - 1 > row0:
                row = lax.broadcasted_iota(jnp.int32, (th, tk), 0) + row0
                col = lax.broadcasted_iota(jnp.int32, (th, tk), 1) + diag_chunk * tk
                s = jnp.where(col <= row, s, NEG_BIG)
            m_new = jnp.maximum(m, jnp.max(s, axis=-1, keepdims=True))
            alpha = jnp.exp2(m - m_new)
            p = jnp.exp2(s - m_new)
            new.append((m_new, alpha * acc + jnp.dot(p.astype(BF16), v, preferred_element_type=F32)))
        return tuple(new)

    def body(c, carry):
        for cc in range(nsub):
            carry = step(pl.multiple_of((c * nsub + cc) * tk, tk), carry, None)
        return carry

    init = tuple((jnp.full((th, 1), NEG_BIG, F32), jnp.zeros((th, 2 * HD_FOX), F32))
                 for _ in range(nsplit))
    carry = lax.fori_loop(0, i, body, init)
    for cc in range(nsub):
        carry = step(pl.multiple_of((i * nsub + cc) * tk, tk), carry, cc)
    acc = jnp.concatenate([a for _, a in carry], axis=0)
    out = acc[:, :HD_FOX] / acc[:, HD_FOX:]
    o_ref[...] = (out * _silu(gf_ref[...].astype(F32))).astype(o_ref.dtype)


def _fox_prompt(q, k, fb, v_aug, gates, batch, seq, tq, tk):
    n = q.shape[0]
    nq = seq // tq
    return pl.pallas_call(
        functools.partial(_fox_prompt_kernel, tk=tk, nsplit=2),
        name="fox_prompt",
        grid=(batch, H_FOX, nq),
        in_specs=[
            pl.BlockSpec((tq, HD_FOX), lambda b, h, i: (b * nq + i, h)),
            pl.BlockSpec((seq, HD_FOX), lambda b, h, i: (b, h)),
            pl.BlockSpec((1, seq, LANES), lambda b, h, i: (b, 0, h)),
            pl.BlockSpec((seq, 2 * HD_FOX), lambda b, h, i: (b, h)),
            pl.BlockSpec((tq, HD_FOX), lambda b, h, i: (b * nq + i, h)),
        ],
        out_specs=pl.BlockSpec((tq, HD_FOX), lambda b, h, i: (b * nq + i, h)),
        out_shape=jax.ShapeDtypeStruct((n, W_FOX), BF16),
        compiler_params=_cparams("arbitrary", "arbitrary", "arbitrary"),
    )(q, k, fb, v_aug, gates)


def _combine_kernel(a_ref, b_ref, ga_ref, gb_ref, h_ref, p_ref, wra_ref, wfa_ref, wout_ref,
                    wpg_ref, wpp_ref, o_ref):
    a = jnp.dot(a_ref[...].astype(BF16), wra_ref[...], preferred_element_type=F32)
    b = jnp.dot(b_ref[...].astype(BF16), wfa_ref[...], preferred_element_type=F32)
    m = _sigmoid(ga_ref[...].astype(F32)) * a + _sigmoid(gb_ref[...].astype(F32)) * b
    h1 = h_ref[...] + jnp.dot(m.astype(BF16), wout_ref[...], preferred_element_type=F32)
    gate = _sigmoid(jnp.dot(h1.astype(BF16), wpg_ref[...], preferred_element_type=F32))
    pe = jnp.dot(p_ref[0].astype(BF16), wpp_ref[...], preferred_element_type=F32)
    o_ref[...] = h1 + gate * pe


def _combine(a_in, b_in, gates, h2d, p3d, layer, wra, wfa, wout, wpg, wpp, tm):
    n, d = h2d.shape
    dp = p3d.shape[-1]
    row = lambda i: (i, 0)
    const = lambda i: (0, 0)
    wspec = lambda w: pl.BlockSpec(w.shape, const, pipeline_mode=pl.Buffered(1))
    return pl.pallas_call(
        _combine_kernel,
        name=f"combine_m{n}",
        grid=(n // tm,),
        in_specs=[
            pl.BlockSpec((tm, W_RET), row),
            pl.BlockSpec((tm, W_FOX), row),
            pl.BlockSpec((tm, d), lambda i: (i, 1)),
            pl.BlockSpec((tm, d), lambda i: (i, 2)),
            pl.BlockSpec((tm, d), row),
            pl.BlockSpec((1, tm, dp), lambda i, _l=layer: (_l, i, 0)),
            wspec(wra), wspec(wfa), wspec(wout), wspec(wpg), wspec(wpp),
        ],
        out_specs=pl.BlockSpec((tm, d), row),
        out_shape=jax.ShapeDtypeStruct((n, d), F32),
        compiler_params=_cparams("arbitrary"),
    )(a_in, b_in, gates, gates, h2d, p3d, wra, wfa, wout, wpg, wpp)


def _ret_sample_kernel(qk_ref, v_ref, gr_ref, st_ref, gn_ref, a_ref, sout_ref, *, gamma):
    qk = qk_ref[0]
    vv = v_ref[0]
    gg = gr_ref[0]
    row8 = lax.broadcasted_iota(jnp.int32, (8, 1), 0)
    for h in range(H_RET):
        q = qk[:, h * DK_RET:(h + 1) * DK_RET]
        k = qk[:, (H_RET + h) * DK_RET:(H_RET + h + 1) * DK_RET]
        cols = slice(h * DV_RET, (h + 1) * DV_RET)
        v = vv[:, cols]
        s_prev = st_ref[0, 0, h]
        q8 = jnp.broadcast_to(q, (8, DK_RET)).astype(BF16)
        cross = jnp.dot(q8, s_prev.astype(BF16), preferred_element_type=F32)[0:1] * gamma[h]
        qk_dot = jnp.sum(q * k, axis=-1, keepdims=True)
        o = qk_dot * v + cross
        k8 = jnp.where(row8 == 0, jnp.broadcast_to(k, (8, DK_RET)), 0.0).astype(BF16)
        v8 = jnp.broadcast_to(v, (8, DV_RET)).astype(BF16)
        outer = lax.dot_general(k8, v8, (((0,), (0,)), ((), ())), preferred_element_type=F32)
        sout_ref[0, h] = gamma[h] * s_prev + outer
        a_ref[0, :, cols] = _groupnorm_gate(o, gn_ref[:, cols], gg[:, cols])


def _ret_sample(qk3, vg3, state, layer, gn_g):
    nb = qk3.shape[0]
    gamma = _ret_tables()[4]
    return pl.pallas_call(
        functools.partial(_ret_sample_kernel, gamma=gamma),
        name="ret_sample",
        grid=(nb,),
        in_specs=[
            pl.BlockSpec((1, 1, 2 * H_RET * DK_RET), lambda b: (b, 0, 0)),
            pl.BlockSpec((1, 1, W_RET), lambda b: (b, 0, 0)),
            pl.BlockSpec((1, 1, W_RET), lambda b: (b, 0, 1)),
            pl.BlockSpec((1, 1, H_RET, DK_RET, DV_RET), lambda b, _l=layer: (_l, b, 0, 0, 0)),
            pl.BlockSpec((1, W_RET), lambda b: (0, 0)),
        ],
        out_specs=[
            pl.BlockSpec((1, 1, W_RET), lambda b: (b, 0, 0)),
            pl.BlockSpec((1, H_RET, DK_RET, DV_RET), lambda b: (b, 0, 0, 0)),
        ],
        out_shape=[
            jax.ShapeDtypeStruct((nb, 1, W_RET), F32),
            jax.ShapeDtypeStruct((nb, H_RET, DK_RET, DV_RET), F32),
        ],
        compiler_params=_cparams("arbitrary"),
    )(qk3, vg3, vg3, state, gn_g.reshape(1, W_RET))


def _head_major_page(ref):
    page = ref.shape[2] // H_FOX
    return jnp.concatenate(
        [ref[0, 0, pl.ds(h, page, stride=H_FOX), :] for h in range(H_FOX)], axis=1).astype(BF16)


def _fox_sample_kernel(pt_ref, q_ref, kn_ref, vn_ref, fn_ref, gf_ref, *rest, group):
    ck_refs, cv_refs, cf_refs = rest[:group], rest[group:2 * group], rest[2 * group:3 * group]
    u_ref, o_ref, qbd_ref, m_ref, l_ref, acc_ref, c_ref = rest[3 * group:]
    j = pl.program_id(1)
    hrow = lax.broadcasted_iota(jnp.int32, (H_FOX, W_FOX), 0)
    hcol = lax.broadcasted_iota(jnp.int32, (H_FOX, W_FOX), 1) // HD_FOX
    diag = hrow == hcol

    @pl.when(j == 0)
    def _():
        qbd_ref[...] = jnp.where(diag, jnp.broadcast_to(q_ref[0], (H_FOX, W_FOX)), 0.0)
        m_ref[...] = jnp.full_like(m_ref, NEG_BIG)
        l_ref[...] = jnp.zeros_like(l_ref)
        acc_ref[...] = jnp.zeros_like(acc_ref)
        c_ref[...] = jnp.zeros_like(c_ref)

    u = u_ref[...]
    carry = c_ref[...]
    biases = []
    for g in range(group):
        lf = cf_refs[g][0, 0]
        hi, mid, lo = _split3(lf)
        suf = (jnp.dot(hi, u, preferred_element_type=F32) + jnp.dot(mid, u, preferred_element_type=F32)
               + jnp.dot(lo, u, preferred_element_type=F32))
        biases.append(suf + (carry + fn_ref[0]))
        carry = carry + jnp.sum(lf, axis=-1, keepdims=True)
    c_ref[...] = carry

    kcat = jnp.concatenate([_head_major_page(r) for r in ck_refs], axis=0)
    s = lax.dot_general(qbd_ref[...].astype(BF16), kcat, (((1,), (1,)), ((), ())),
                        preferred_element_type=F32) + jnp.concatenate(biases, axis=1) * LOG2E
    m_new = jnp.maximum(m_ref[...], jnp.max(s, axis=-1, keepdims=True))
    alpha = jnp.exp2(m_ref[...] - m_new)
    p = jnp.exp2(s - m_new)
    l_ref[...] = alpha * l_ref[...] + jnp.sum(p, axis=-1, keepdims=True)
    vcat = jnp.concatenate([_head_major_page(r) for r in cv_refs], axis=0)
    acc_ref[...] = alpha * acc_ref[...] + jnp.dot(p.astype(BF16), vcat, preferred_element_type=F32)
    m_ref[...] = m_new

    @pl.when(j == pl.num_programs(1) - 1)
    def _():
        s_new = jnp.sum(qbd_ref[...] * kn_ref[0], axis=-1, keepdims=True)
        m_f = jnp.maximum(m_ref[...], s_new)
        al = jnp.exp2(m_ref[...] - m_f)
        p_new = jnp.exp2(s_new - m_f)
        l_f = al * l_ref[...] + p_new
        full = (al * acc_ref[...] + p_new * vn_ref[0]) / l_f
        out = jnp.sum(jnp.where(diag, full, 0.0), axis=0, keepdims=True)
        o_ref[0] = out * _silu(gf_ref[0])


def _fox_sample(q3, kn3, vn3, fn_col, gates3, cache_kf, cache_vf, cache_ft, page_table, layer, group):
    nb, n_pages = page_table.shape
    page = cache_ft.shape[3]
    u = jnp.asarray(np.tril(np.ones((page, page), np.float32), -1), BF16)
    last = n_pages - 1
    vec = lambda b, j, pt: (b, 0, 0)

    def pg(g):
        return lambda b, j, pt, _l=layer, _g=g: (_l, pt[b, last - (j * group + _g)], 0, 0)

    kv_specs = [pl.BlockSpec((1, 1, page * H_FOX, HD_FOX), pg(g)) for g in range(group)]
    f_specs = [pl.BlockSpec((1, 1, H_FOX, page), pg(g)) for g in range(group)]
    grid_spec = pltpu.PrefetchScalarGridSpec(
        num_scalar_prefetch=1,
        grid=(nb, n_pages // group),
        in_specs=[
            pl.BlockSpec((1, 1, W_FOX), vec),
            pl.BlockSpec((1, 1, W_FOX), vec),
            pl.BlockSpec((1, 1, W_FOX), vec),
            pl.BlockSpec((1, H_FOX, 1), vec),
            pl.BlockSpec((1, 1, W_FOX), vec),
        ] + kv_specs + kv_specs + f_specs + [pl.BlockSpec((page, page), lambda b, j, pt: (0, 0))],
        out_specs=pl.BlockSpec((1, 1, W_FOX), vec),
        scratch_shapes=[
            pltpu.VMEM((H_FOX, W_FOX), F32),
            pltpu.VMEM((H_FOX, 1), F32),
            pltpu.VMEM((H_FOX, 1), F32),
            pltpu.VMEM((H_FOX, W_FOX), F32),
            pltpu.VMEM((H_FOX, 1), F32),
        ],
    )
    return pl.pallas_call(
        functools.partial(_fox_sample_kernel, group=group),
        name="fox_sample",
        grid_spec=grid_spec,
        out_shape=jax.ShapeDtypeStruct((nb, 1, W_FOX), F32),
        compiler_params=_cparams("arbitrary", "arbitrary"),
    )(page_table, q3, kn3, vn3, fn_col, gates3, *([cache_kf] * group), *([cache_vf] * group),
      *([cache_ft] * group), u)


def _rotary_tables(pos, k_scale):
    half = DK_RET // 2
    inv = ROPE_BASE ** (-np.arange(half, dtype=np.float64) / half)
    ang = np.asarray(pos, np.float64)[:, None] * inv[None, :]
    cos = np.concatenate([np.cos(ang), np.cos(ang)], axis=-1)
    sin = np.concatenate([-np.sin(ang), np.sin(ang)], axis=-1)
    return (jnp.asarray(np.stack([cos, cos * k_scale]), F32),
            jnp.asarray(np.stack([sin, sin * k_scale]), F32))


def _layer_weights(l, norm_g, w_in, b_forget, q_norm_g, k_norm_g, ret_gn_g, w_branch_ret, w_branch_fox,
                   w_out, w_ple_gate, w_ple_proj):
    w = w_in[l]
    bf = lambda a: a.astype(BF16)
    pad = LANES - H_FOX
    return {
        "norm_g": norm_g[l],
        "w_qk": bf(w[:, _QR0:_VR0]),
        "w_vg": bf(w[:, _VR0:_QF0]),
        "w_qf": bf(w[:, _QF0:_KF0]),
        "w_kf": bf(w[:, _KF0:_VF0]),
        "w_vf": bf(w[:, _VF0:_FL0]),
        "w_fl": bf(jnp.pad(w[:, _FL0:_GF0], ((0, 0), (0, pad)))),
        "b_fl": jnp.pad(b_forget[l], (0, pad)).reshape(1, LANES).astype(F32),
        "w_gates": bf(w[:, _GF0:]),
        "q_norm_g": q_norm_g[l],
        "k_norm_g": k_norm_g[l],
        "ret_gn_g": ret_gn_g[l],
        "wra": bf(w_branch_ret[l]),
        "wfa": bf(w_branch_fox[l]),
        "wout": bf(w_out[l]),
        "wpg": bf(w_ple_gate[l]),
        "wpp": bf(w_ple_proj[l]),
    }


def kernel(x_prompt, x_sample, cache_k, cache_v, cache_logf, state_ret, page_table, p_prompt, p_sample,
           norm_g, w_in, b_forget, q_norm_g, k_norm_g, ret_gn_g, w_branch_ret, w_branch_fox, w_out,
           w_ple_gate, w_ple_proj):
    batch, seq, d = x_prompt.shape
    nb, tdec, _ = x_sample.shape
    depth, n_pool, page = cache_k.shape[0], cache_k.shape[1], cache_k.shape[2]
    past_len = page_table.shape[1] * page
    n = batch * seq
    ns = nb * tdec
    assert tdec == 1, "decode group handles one new position per sequence"

    tm_p = min(256, seq)
    rb = min(512, seq)
    tq = min(1024, seq)
    tk = min(512, seq)
    tm_c = min(512, seq)

    rot_p = _rotary_tables(np.arange(seq), DK_RET ** -0.5)
    rot_s = _rotary_tables(np.tile(past_len + np.arange(tdec), nb), DK_RET ** -0.5)

    ckf = cache_k.reshape(depth, n_pool, page * H_FOX, HD_FOX)
    cvf = cache_v.reshape(depth, n_pool, page * H_FOX, HD_FOX)
    group = math.gcd(page_table.shape[1], 8)
    cft = jnp.swapaxes(cache_logf, 2, 3)
    p_p = p_prompt.reshape(depth, n, p_prompt.shape[-1])
    p_s = p_sample.reshape(depth, ns, p_sample.shape[-1])

    hp = x_prompt.reshape(n, d)
    hs = x_sample.reshape(ns, d)
    prev_p = prev_s = None
    fp_l, sp_l, fs_l, ss_l = [], [], [], []
    for l in range(depth):
        lw = _layer_weights(l, norm_g, w_in, b_forget, q_norm_g, k_norm_g, ret_gn_g, w_branch_ret,
                            w_branch_fox, w_out, w_ple_gate, w_ple_proj)
        pr = _proj_fused(hp, lw, rot_p, tm=tm_p, layer=l, depth=depth, prev=prev_p, act_dtype=BF16)
        prev_p = (pr["k_stack"], pr["v_stack"])
        logf = pr["logf"]
        fp_l.append(logf[:, :H_FOX])
        fb = _fbias(logf.reshape(batch, seq, LANES))
        a_in, s_fin = _ret_prompt(pr["qk_r"], pr["vg_r"], lw["ret_gn_g"], batch, seq, rb)
        sp_l.append(s_fin)
        b_in = _fox_prompt(pr["q_f"], pr["k_f"], fb, pr["v_aug"], pr["gates"], batch, seq, tq, tk)
        hp = _combine(a_in, b_in, pr["gates"], hp, p_p, l, lw["wra"], lw["wfa"], lw["wout"],
                      lw["wpg"], lw["wpp"], tm_c)
        sr = _proj_fused(hs, lw, rot_s, tm=ns, layer=l, depth=depth, prev=prev_s, act_dtype=F32)
        prev_s = (sr["k_stack"], sr["v_stack"])
        logf_s = sr["logf"][:, :H_FOX]
        fs_l.append(logf_s)
        a_s, s_new = _ret_sample(sr["qk_r"].reshape(ns, 1, -1), sr["vg_r"].reshape(ns, 1, -1),
                                 state_ret, l, lw["ret_gn_g"])
        ss_l.append(s_new)
        v_new = sr["v_aug"].reshape(ns, H_FOX, 2, HD_FOX)[:, :, 0, :].reshape(ns, 1, W_FOX)
        b_s = _fox_sample(sr["q_f"].reshape(ns, 1, -1), sr["k_f"].reshape(ns, 1, -1), v_new,
                          logf_s.reshape(ns, H_FOX, 1), sr["gates"].reshape(ns, 1, -1),
                          ckf, cvf, cft, page_table, l, group)
        hs = _combine(a_s.reshape(ns, -1), b_s.reshape(ns, -1), sr["gates"], hs, p_s, l, lw["wra"],
                      lw["wfa"], lw["wout"], lw["wpg"], lw["wpp"], ns)

    return (
        hp.reshape(batch, seq, d),
        hs.reshape(nb, tdec, d),
        prev_p[0].reshape(depth, batch, seq, H_FOX, HD_FOX),
        prev_p[1].reshape(depth, batch, seq, H_FOX, HD_FOX),
        jnp.stack(fp_l).reshape(depth, batch, seq, H_FOX),
        jnp.stack(sp_l),
        prev_s[0].reshape(depth, nb, tdec, H_FOX, HD_FOX),
        prev_s[1].reshape(depth, nb, tdec, H_FOX, HD_FOX),
        jnp.stack(fs_l).reshape(depth, nb, tdec, H_FOX),
        jnp.stack(ss_l),
    )
```

```python
import functools
import math

import numpy as np
import jax
import jax.numpy as jnp
from jax import lax
from jax.experimental import pallas as pl
from jax.experimental.pallas import tpu as pltpu

F32 = jnp.float32
BF16 = jnp.bfloat16

H_RET = 4
DK_RET = 128
DV_RET = 256
RET_CHUNK = 128
W_RET = H_RET * DV_RET
H_FOX = 8
HD_FOX = 128
W_FOX = H_FOX * HD_FOX
ROPE_BASE = 10000.0
EPS = 1e-6
LOG2E = math.log2(math.e)
LANES = 128
NEG_BIG = -1e30
VMEM_LIMIT = 56 * 1024 * 1024
PROJ_COLS = 512

_QR0 = 0
_KR0 = _QR0 + H_RET * DK_RET
_VR0 = _KR0 + H_RET * DK_RET
_GR0 = _VR0 + W_RET
_QF0 = _GR0 + W_RET
_KF0 = _QF0 + W_FOX
_VF0 = _KF0 + W_FOX
_FL0 = _VF0 + W_FOX
_GF0 = _FL0 + H_FOX
_GA0 = _GF0 + W_FOX


def _cparams(*sem):
    return pltpu.CompilerParams(dimension_semantics=sem, vmem_limit_bytes=VMEM_LIMIT)


def _sigmoid(x):
    return 1.0 / (1.0 + jnp.exp(-x))


def _silu(x):
    return x * _sigmoid(x)


def _split3(x):
    hi = x.astype(BF16).astype(F32)
    r1 = x - hi
    mid = r1.astype(BF16).astype(F32)
    lo = (r1 - mid).astype(BF16).astype(F32)
    return hi, mid, lo


def _layer_spec(a, layer, width=None, col=0, resident=False):
    width = a.shape[2] if width is None else width
    return pl.BlockSpec((1, a.shape[1], width), lambda *_, _l=layer, _c=col: (_l, 0, _c),
                        pipeline_mode=pl.Buffered(1) if resident else None)


def _proj_fused_kernel(x_ref, g_ref, cos_ref, sin_ref, hgq_ref, hgk_ref, bfl_ref,
                       wqk_ref, wvr_ref, wgr_ref, wqf_ref, wkf_ref, wvf_ref, wg_ref, wfl_ref, *rest,
                       n_alias):
    (oqk_ref, ovg_ref, oqf_ref, okst_ref, okb_ref, ovst_ref, ovb_ref, olf_ref, og_ref,
     xn_ref) = rest[n_alias:]
    x = x_ref[...]
    ms = jnp.mean(x * x, axis=-1, keepdims=True)
    xn_ref[...] = (x * lax.rsqrt(ms + EPS) * g_ref[0]).astype(BF16)

    def seg(w_ref, c0):
        return jnp.dot(xn_ref[...], w_ref[0, :, c0:c0 + PROJ_COLS], preferred_element_type=F32)

    heads = PROJ_COLS // LANES
    for c0 in range(0, wqk_ref.shape[2], PROJ_COLS):
        z = seg(wqk_ref, c0)
        which = 0 if c0 < H_RET * DK_RET else 1
        cos, sin = cos_ref[which], sin_ref[which]
        for hh in range(heads):
            zh = z[:, hh * LANES:(hh + 1) * LANES]
            rot = pltpu.roll(zh, LANES // 2, axis=1)
            oqk_ref[:, c0 + hh * LANES:c0 + (hh + 1) * LANES] = (zh * cos + rot * sin).astype(oqk_ref.dtype)
    for w_ref, o_ref, o0 in ((wvr_ref, ovg_ref, 0), (wgr_ref, ovg_ref, W_RET), (wg_ref, og_ref, 0)):
        for c0 in range(0, w_ref.shape[2], PROJ_COLS):
            o_ref[:, o0 + c0:o0 + c0 + PROJ_COLS] = seg(w_ref, c0).astype(o_ref.dtype)
    hgq = hgq_ref[0] * (HD_FOX ** -0.5 * LOG2E)
    hgk = hgk_ref[0]
    for c0 in range(0, W_FOX, PROJ_COLS):
        zq = seg(wqf_ref, c0)
        zk = seg(wkf_ref, c0)
        for hh in range(heads):
            cols = slice(c0 + hh * LANES, c0 + (hh + 1) * LANES)
            zh = zq[:, hh * LANES:(hh + 1) * LANES]
            yq = zh * lax.rsqrt(jnp.mean(zh * zh, axis=-1, keepdims=True) + EPS) * hgq
            oqf_ref[:, cols] = yq.astype(oqf_ref.dtype)
            zh = zk[:, hh * LANES:(hh + 1) * LANES]
            yk = zh * lax.rsqrt(jnp.mean(zh * zh, axis=-1, keepdims=True) + EPS) * hgk
            okst_ref[0, :, cols] = yk
            okb_ref[:, cols] = yk.astype(okb_ref.dtype)
    ones = jnp.ones((x.shape[0], LANES), ovb_ref.dtype)
    for c0 in range(0, W_FOX, PROJ_COLS):
        z = seg(wvf_ref, c0)
        ovst_ref[0, :, c0:c0 + PROJ_COLS] = z
        for hh in range(heads):
            h = c0 // LANES + hh
            ovb_ref[:, 2 * h * LANES:(2 * h + 1) * LANES] = z[:, hh * LANES:(hh + 1) * LANES].astype(ovb_ref.dtype)
            ovb_ref[:, (2 * h + 1) * LANES:(2 * h + 2) * LANES] = ones
    z = jnp.dot(xn_ref[...], wfl_ref[0], preferred_element_type=F32) + bfl_ref[0]
    olf_ref[...] = jnp.minimum(z, 0.0) - jnp.log1p(jnp.exp(-jnp.abs(z)))


def _proj_fused(x2d, wts, rot, *, tm, layer, depth, prev, act_dtype):
    m, d = x2d.shape
    cos, sin = rot
    nt = cos.shape[1] // tm
    row = lambda i: (i, 0)
    tab = pl.BlockSpec((2, tm, LANES), lambda i: (0, i % nt, 0))
    lay = functools.partial(_layer_spec, layer=layer)
    wa, wg, wfl = wts["w_a"], wts["w_gates"], wts["w_fl"]
    n_seg = wa.shape[2] // W_FOX
    args = [x2d, wts["norm_g"], cos, sin, wts["q_norm_g"], wts["k_norm_g"], wts["b_fl"]]
    args += [wa] * n_seg + [wg, wfl]
    in_specs = [pl.BlockSpec((tm, d), row), lay(wts["norm_g"]), tab, tab, lay(wts["q_norm_g"]),
                lay(wts["k_norm_g"]), lay(wts["b_fl"])]
    in_specs += [lay(wa, width=W_FOX, col=c, resident=True) for c in range(n_seg)]
    in_specs += [lay(wg, resident=True), lay(wfl, resident=True)]
    aliases = {}
    if prev is not None:
        aliases = {len(args): 3, len(args) + 1: 5}
        args += list(prev)
        in_specs += [pl.BlockSpec(memory_space=pl.ANY)] * 2
    flat = lambda cols, dt: (jax.ShapeDtypeStruct((m, cols), dt), pl.BlockSpec((tm, cols), row))
    stack = lambda cols: (jax.ShapeDtypeStruct((depth, m, cols), F32),
                          pl.BlockSpec((1, tm, cols), lambda i, _l=layer: (_l, i, 0)))
    outs = [flat(2 * H_RET * DK_RET, act_dtype), flat(2 * W_RET, act_dtype), flat(W_FOX, act_dtype),
            stack(W_FOX), flat(W_FOX, act_dtype), stack(W_FOX), flat(2 * W_FOX, act_dtype),
            flat(LANES, F32), flat(3 * d, act_dtype)]
    res = pl.pallas_call(
        functools.partial(_proj_fused_kernel, n_alias=len(aliases)),
        name=f"proj_fused_m{m}",
        grid=(m // tm,),
        in_specs=in_specs,
        out_specs=[o[1] for o in outs],
        out_shape=[o[0] for o in outs],
        scratch_shapes=[pltpu.VMEM((tm, d), BF16)],
        input_output_aliases=aliases,
        compiler_params=_cparams("arbitrary"),
    )(*args)
    names = ("qk_r", "vg_r", "q_f", "k_stack", "k_f", "v_stack", "v_aug", "logf", "gates")
    return dict(zip(names, res))


def _fbias_kernel(x_ref, tri_ref, sel_ref, o_ref):
    c = tri_ref.shape[0]
    tri = tri_ref[...]
    sel = sel_ref[...]
    carry = jnp.zeros((1, x_ref.shape[2]), F32)
    for k in range(x_ref.shape[1] // c):
        rows = slice(k * c, (k + 1) * c)
        hi, mid, lo = _split3(x_ref[0, rows, :])
        f = (jnp.dot(tri, hi.astype(BF16), preferred_element_type=F32)
             + jnp.dot(tri, mid.astype(BF16), preferred_element_type=F32)
             + jnp.dot(tri, lo.astype(BF16), preferred_element_type=F32)) + carry
        pieces = jnp.concatenate(_split3(f * (-LOG2E)), axis=1).astype(BF16)
        o_ref[0, rows, :] = jnp.dot(pieces, sel, preferred_element_type=F32).astype(o_ref.dtype)
        carry = f[c - 1:c, :]


def _fbias(logf3d, chunk=128):
    b, s, w = logf3d.shape
    tri = jnp.asarray(np.tril(np.ones((chunk, chunk), np.float32)), BF16)
    sel = np.zeros((3 * LANES, LANES), np.float32)
    for piece in range(3):
        for h in range(H_FOX):
            sel[piece * LANES + h, piece * H_FOX + h] = 1.0
    sel = jnp.asarray(sel, BF16)
    return pl.pallas_call(
        _fbias_kernel,
        name="fbias",
        grid=(b,),
        in_specs=[pl.BlockSpec((1, s, w), lambda i: (i, 0, 0)),
                  pl.BlockSpec((chunk, chunk), lambda i: (0, 0)),
                  pl.BlockSpec(sel.shape, lambda i: (0, 0))],
        out_specs=pl.BlockSpec((1, s, LANES), lambda i: (i, 0, 0)),
        out_shape=jax.ShapeDtypeStruct((b, s, LANES), BF16),
        compiler_params=_cparams("arbitrary"),
    )(logf3d, tri, sel)


def _groupnorm_gate(o, gain, gate):
    mu = jnp.mean(o, axis=-1, keepdims=True)
    d = o - mu
    var = jnp.mean(d * d, axis=-1, keepdims=True)
    return d * lax.rsqrt(var + EPS) * gain * _silu(gate)


def _ret_prompt_kernel(qk_ref, v_ref, gr_ref, dmat_ref, qdec_ref, kdec_ref, sdec_ref, gn_ref,
                       a_ref, sout_ref, s_ref):
    r = pl.program_id(1)

    @pl.when(r == 0)
    def _():
        s_ref[...] = jnp.zeros_like(s_ref)

    rb = qk_ref.shape[0]
    for c in range(rb // RET_CHUNK):
        rows = slice(c * RET_CHUNK, (c + 1) * RET_CHUNK)
        for h in range(H_RET):
            q = qk_ref[rows, h * DK_RET:(h + 1) * DK_RET]
            k = qk_ref[rows, (H_RET + h) * DK_RET:(H_RET + h + 1) * DK_RET]
            v = v_ref[rows, h * DV_RET:(h + 1) * DV_RET]
            s_prev = s_ref[h]
            sc = lax.dot_general(q, k, (((1,), (1,)), ((), ())), preferred_element_type=F32) * dmat_ref[h]
            inner = jnp.dot(sc.astype(BF16), v, preferred_element_type=F32)
            cross = jnp.dot(q, s_prev.astype(BF16), preferred_element_type=F32) * qdec_ref[h]
            o = inner + cross
            kd = (k.astype(F32) * kdec_ref[h]).astype(BF16)
            s_ref[h] = sdec_ref[h] * s_prev + lax.dot_general(
                kd, v, (((0,), (0,)), ((), ())), preferred_element_type=F32)
            cols = slice(h * DV_RET, (h + 1) * DV_RET)
            a_ref[rows, cols] = _groupnorm_gate(
                o, gn_ref[0, :, cols], gr_ref[rows, cols].astype(F32)).astype(a_ref.dtype)

    @pl.when(r == pl.num_programs(1) - 1)
    def _():
        sout_ref[0] = s_ref[...]


def _ret_tables():
    lg = np.log1p(-np.exp2(-5.0 - np.arange(H_RET, dtype=np.float64)))
    idx = np.arange(RET_CHUNK, dtype=np.float64)
    rel = idx[:, None] - idx[None, :]
    dmat = np.where(rel >= 0, np.exp(lg[:, None, None] * np.maximum(rel, 0.0)[None]), 0.0)
    qdec = np.exp(lg[:, None] * (idx[None, :] + 1.0))
    kdec = np.exp(lg[:, None] * (RET_CHUNK - 1.0 - idx)[None, :])
    sdec = np.exp(lg * RET_CHUNK)
    gamma = np.exp(lg)
    return (jnp.asarray(dmat, F32),
            jnp.asarray(np.broadcast_to(qdec[:, :, None], (H_RET, RET_CHUNK, DV_RET)), F32),
            jnp.asarray(np.broadcast_to(kdec[:, :, None], (H_RET, RET_CHUNK, DK_RET)), F32),
            jnp.asarray(np.broadcast_to(sdec[:, None, None], (H_RET, 1, DV_RET)), F32),
            [float(x) for x in gamma])


def _ret_prompt(qk, vg, gn_g, layer, batch, seq, rb):
    n = qk.shape[0]
    nrb = seq // rb
    dmat, qdec, kdec, sdec, _ = _ret_tables()
    full3 = lambda b, r: (0, 0, 0)
    return pl.pallas_call(
        _ret_prompt_kernel,
        name="ret_prompt",
        grid=(batch, nrb),
        in_specs=[
            pl.BlockSpec((rb, 2 * H_RET * DK_RET), lambda b, r: (b * nrb + r, 0)),
            pl.BlockSpec((rb, W_RET), lambda b, r: (b * nrb + r, 0)),
            pl.BlockSpec((rb, W_RET), lambda b, r: (b * nrb + r, 1)),
            pl.BlockSpec(dmat.shape, full3),
            pl.BlockSpec(qdec.shape, full3),
            pl.BlockSpec(kdec.shape, full3),
            pl.BlockSpec(sdec.shape, full3),
            _layer_spec(gn_g, layer),
        ],
        out_specs=[
            pl.BlockSpec((rb, W_RET), lambda b, r: (b * nrb + r, 0)),
            pl.BlockSpec((1, H_RET, DK_RET, DV_RET), lambda b, r: (b, 0, 0, 0)),
        ],
        out_shape=[
            jax.ShapeDtypeStruct((n, W_RET), BF16),
            jax.ShapeDtypeStruct((batch, H_RET, DK_RET, DV_RET), F32),
        ],
        scratch_shapes=[pltpu.VMEM((H_RET, DK_RET, DV_RET), F32)],
        compiler_params=_cparams("arbitrary", "arbitrary"),
    )(qk, vg, vg, dmat, qdec, kdec, sdec, gn_g)


def _fox_prompt_kernel(q_ref, k_ref, fb_ref, v_ref, gf_ref, o_ref, *, tk, nsplit):
    h = pl.program_id(1)
    i = pl.program_id(2)
    tq = q_ref.shape[0]
    lane = lax.broadcasted_iota(jnp.int32, (tq, LANES), 1)
    mine = (lane == h) | (lane == h + H_FOX) | (lane == h + 2 * H_FOX)
    q = jnp.concatenate([q_ref[...], jnp.where(mine, 1.0, 0.0).astype(q_ref.dtype)], axis=1)
    nsub = tq // tk
    th = tq // nsplit

    def step(off, carry, diag_chunk):
        k = jnp.concatenate([k_ref[pl.ds(off, tk), :], fb_ref[0, pl.ds(off, tk), :]], axis=1)
        v = v_ref[pl.ds(off, tk), :]
        new = []
        for part, (m, acc) in enumerate(carry):
            row0 = part * th
            if diag_chunk is not None and diag_chunk * tk > row0 + th - 1:
                new.append((m, acc))
                continue
            s = lax.dot_general(q[row0:row0 + th], k, (((1,), (1,)), ((), ())),
                                preferred_element_type=F32)
            if diag_chunk is not None and diag_chunk * tk + tk - 1 > row0:
                row = lax.broadcasted_iota(jnp.int32, (th, tk), 0) + row0
                col = lax.broadcasted_iota(jnp.int32, (th, tk), 1) + diag_chunk * tk
                s = jnp.where(col <= row, s, NEG_BIG)
            m_new = jnp.maximum(m, jnp.max(s, axis=-1, keepdims=True))
            alpha = jnp.exp2(m - m_new)
            p = jnp.exp2(s - m_new)
            new.append((m_new, alpha * acc + jnp.dot(p.astype(BF16), v, preferred_element_type=F32)))
        return tuple(new)

    def body(c, carry):
        for cc in range(nsub):
            carry = step(pl.multiple_of((c * nsub + cc) * tk, tk), carry, None)
        return carry

    init = tuple((jnp.full((th, 1), NEG_BIG, F32), jnp.zeros((th, 2 * HD_FOX), F32))
                 for _ in range(nsplit))
    carry = lax.fori_loop(0, i, body, init)
    for cc in range(nsub):
        carry = step(pl.multiple_of((i * nsub + cc) * tk, tk), carry, cc)
    acc = jnp.concatenate([a for _, a in carry], axis=0)
    out = acc[:, :HD_FOX] / acc[:, HD_FOX:]
    o_ref[...] = (out * _silu(gf_ref[...].astype(F32))).astype(o_ref.dtype)


def _fox_prompt(q, k, fb, v_aug, gates, batch, seq, tq, tk):
    n = q.shape[0]
    nq = seq // tq
    return pl.pallas_call(
        functools.partial(_fox_prompt_kernel, tk=tk, nsplit=tq // tk),
        name="fox_prompt",
        grid=(batch, H_FOX, nq),
        in_specs=[
            pl.BlockSpec((tq, HD_FOX), lambda b, h, i: (b * nq + i, h)),
            pl.BlockSpec((seq, HD_FOX), lambda b, h, i: (b, h)),
            pl.BlockSpec((1, seq, LANES), lambda b, h, i: (b, 0, 0)),
            pl.BlockSpec((seq, 2 * HD_FOX), lambda b, h, i: (b, h)),
            pl.BlockSpec((tq, HD_FOX), lambda b, h, i: (b * nq + i, h)),
        ],
        out_specs=pl.BlockSpec((tq, HD_FOX), lambda b, h, i: (b * nq + i, h)),
        out_shape=jax.ShapeDtypeStruct((n, W_FOX), BF16),
        compiler_params=_cparams("arbitrary", "arbitrary", "arbitrary"),
    )(q, k, fb, v_aug, gates)


def _combine_kernel(a_ref, b_ref, ga_ref, gb_ref, h_ref, p_ref, wra_ref, wfa_ref, wout_ref,
                    wpg_ref, wpp_ref, o_ref):
    a = jnp.dot(a_ref[...].astype(BF16), wra_ref[0], preferred_element_type=F32)
    b = jnp.dot(b_ref[...].astype(BF16), wfa_ref[0], preferred_element_type=F32)
    m = _sigmoid(ga_ref[...].astype(F32)) * a + _sigmoid(gb_ref[...].astype(F32)) * b
    h1 = h_ref[...] + jnp.dot(m.astype(BF16), wout_ref[0], preferred_element_type=F32)
    gate = _sigmoid(jnp.dot(h1.astype(BF16), wpg_ref[0], preferred_element_type=F32))
    pe = jnp.dot(p_ref[0].astype(BF16), wpp_ref[0], preferred_element_type=F32)
    o_ref[...] = h1 + gate * pe


def _combine(a_in, b_in, gates, h2d, p3d, layer, wts, tm):
    n, d = h2d.shape
    dp = p3d.shape[-1]
    row = lambda i: (i, 0)
    wspec = lambda name: _layer_spec(wts[name], layer, resident=True)
    names = ("wra", "wfa", "wout", "wpg", "wpp")
    return pl.pallas_call(
        _combine_kernel,
        name=f"combine_m{n}",
        grid=(n // tm,),
        in_specs=[
            pl.BlockSpec((tm, W_RET), row),
            pl.BlockSpec((tm, W_FOX), row),
            pl.BlockSpec((tm, d), lambda i: (i, 1)),
            pl.BlockSpec((tm, d), lambda i: (i, 2)),
            pl.BlockSpec((tm, d), row),
            pl.BlockSpec((1, tm, dp), lambda i, _l=layer: (_l, i, 0)),
        ] + [wspec(k) for k in names],
        out_specs=pl.BlockSpec((tm, d), row),
        out_shape=jax.ShapeDtypeStruct((n, d), F32),
        compiler_params=_cparams("arbitrary"),
    )(a_in, b_in, gates, gates, h2d, p3d, *[wts[k] for k in names])


def _ret_sample_kernel(qk_ref, v_ref, gr_ref, st_ref, gn_ref, a_ref, sout_ref, *, gamma):
    qk = qk_ref[0]
    vv = v_ref[0]
    gg = gr_ref[0]
    row8 = lax.broadcasted_iota(jnp.int32, (8, 1), 0)
    for h in range(H_RET):
        q = qk[:, h * DK_RET:(h + 1) * DK_RET]
        k = qk[:, (H_RET + h) * DK_RET:(H_RET + h + 1) * DK_RET]
        cols = slice(h * DV_RET, (h + 1) * DV_RET)
        v = vv[:, cols]
        s_prev = st_ref[0, 0, h]
        q8 = jnp.broadcast_to(q, (8, DK_RET)).astype(BF16)
        cross = jnp.dot(q8, s_prev.astype(BF16), preferred_element_type=F32)[0:1] * gamma[h]
        qk_dot = jnp.sum(q * k, axis=-1, keepdims=True)
        o = qk_dot * v + cross
        k8 = jnp.where(row8 == 0, jnp.broadcast_to(k, (8, DK_RET)), 0.0).astype(BF16)
        v8 = jnp.broadcast_to(v, (8, DV_RET)).astype(BF16)
        outer = lax.dot_general(k8, v8, (((0,), (0,)), ((), ())), preferred_element_type=F32)
        sout_ref[0, h] = gamma[h] * s_prev + outer
        a_ref[0, :, cols] = _groupnorm_gate(o, gn_ref[0, :, cols], gg[:, cols])


def _ret_sample(qk3, vg3, state, layer, gn_g):
    nb = qk3.shape[0]
    gamma = _ret_tables()[4]
    return pl.pallas_call(
        functools.partial(_ret_sample_kernel, gamma=gamma),
        name="ret_sample",
        grid=(nb,),
        in_specs=[
            pl.BlockSpec((1, 1, 2 * H_RET * DK_RET), lambda b: (b, 0, 0)),
            pl.BlockSpec((1, 1, W_RET), lambda b: (b, 0, 0)),
            pl.BlockSpec((1, 1, W_RET), lambda b: (b, 0, 1)),
            pl.BlockSpec((1, 1, H_RET, DK_RET, DV_RET), lambda b, _l=layer: (_l, b, 0, 0, 0)),
            _layer_spec(gn_g, layer),
        ],
        out_specs=[
            pl.BlockSpec((1, 1, W_RET), lambda b: (b, 0, 0)),
            pl.BlockSpec((1, H_RET, DK_RET, DV_RET), lambda b: (b, 0, 0, 0)),
        ],
        out_shape=[
            jax.ShapeDtypeStruct((nb, 1, W_RET), F32),
            jax.ShapeDtypeStruct((nb, H_RET, DK_RET, DV_RET), F32),
        ],
        compiler_params=_cparams("arbitrary"),
    )(qk3, vg3, vg3, state, gn_g)


def _fox_sample_kernel(pt_ref, q_ref, kn_ref, vn_ref, fn_ref, gf_ref, *rest, group):
    ck_refs, cv_refs, cf_refs = rest[:group], rest[group:2 * group], rest[2 * group:3 * group]
    u_ref, e_ref, o_ref, m_ref, l_ref, acc_ref, c_ref = rest[3 * group:]
    j = pl.program_id(1)
    half = group // 2

    @pl.when(j == 0)
    def _():
        m_ref[...] = jnp.full_like(m_ref, NEG_BIG)
        l_ref[...] = jnp.zeros_like(l_ref)
        acc_ref[...] = jnp.zeros_like(acc_ref)
        c_ref[...] = jnp.zeros_like(c_ref)

    u = u_ref[...]
    carry = c_ref[...]
    pieces = []
    for g in range(group):
        lf = cf_refs[g][0, 0]
        hi, mid, lo = _split3(lf)
        suf = (jnp.dot(hi.astype(BF16), u, preferred_element_type=F32)
               + jnp.dot(mid.astype(BF16), u, preferred_element_type=F32)
               + jnp.dot(lo.astype(BF16), u, preferred_element_type=F32))
        pieces.extend(_split3((suf + (carry + fn_ref[0])) * LOG2E))
        carry = carry + jnp.sum(lf, axis=-1, keepdims=True)
    c_ref[...] = carry
    spread = jnp.dot(jnp.concatenate(pieces, axis=0).astype(BF16), e_ref[...],
                     preferred_element_type=F32)
    bias = [spread[3 * H_FOX * g:3 * H_FOX * g + H_FOX]
            + spread[3 * H_FOX * g + H_FOX:3 * H_FOX * g + 2 * H_FOX]
            + spread[3 * H_FOX * g + 2 * H_FOX:3 * H_FOX * (g + 1)] for g in range(group)]
    bias2 = jnp.concatenate([jnp.concatenate(bias[0::2], axis=1),
                             jnp.concatenate(bias[1::2], axis=1)], axis=0)

    q = q_ref[0]
    zero = jnp.zeros_like(q)
    q2 = jnp.concatenate([jnp.concatenate([q, zero], axis=1),
                          jnp.concatenate([zero, q], axis=1)], axis=0).astype(BF16)
    kall = jnp.concatenate(
        [jnp.concatenate([ck_refs[2 * t][0, 0], ck_refs[2 * t + 1][0, 0]], axis=1) for t in range(half)],
        axis=0).astype(BF16)
    s = lax.dot_general(q2, kall, (((1,), (1,)), ((), ())), preferred_element_type=F32) + bias2
    row = lax.broadcasted_iota(jnp.int32, s.shape, 0)
    col = lax.broadcasted_iota(jnp.int32, s.shape, 1)
    s = jnp.where((col & (H_FOX - 1)) == (row & (H_FOX - 1)), s, NEG_BIG)
    mx = jnp.max(s, axis=-1, keepdims=True)
    m_new = jnp.maximum(m_ref[...], jnp.maximum(mx[:H_FOX], mx[H_FOX:]))
    alpha = jnp.exp2(m_ref[...] - m_new)
    p = jnp.exp2(s - jnp.concatenate([m_new, m_new], axis=0))
    ps = jnp.sum(p, axis=-1, keepdims=True)
    l_ref[...] = alpha * l_ref[...] + ps[:H_FOX] + ps[H_FOX:]
    vall = jnp.concatenate(
        [jnp.concatenate([cv_refs[2 * t][0, 0], cv_refs[2 * t + 1][0, 0]], axis=1) for t in range(half)],
        axis=0).astype(BF16)
    pv = jnp.dot(p.astype(BF16), vall, preferred_element_type=F32)
    acc_ref[...] = alpha * acc_ref[...] + pv[:H_FOX, :HD_FOX] + pv[H_FOX:, HD_FOX:]
    m_ref[...] = m_new

    @pl.when(j == pl.num_programs(1) - 1)
    def _():
        s_new = jnp.sum(q * kn_ref[0], axis=-1, keepdims=True)
        m_f = jnp.maximum(m_ref[...], s_new)
        al = jnp.exp2(m_ref[...] - m_f)
        p_new = jnp.exp2(s_new - m_f)
        l_f = al * l_ref[...] + p_new
        o_ref[0] = (al * acc_ref[...] + p_new * vn_ref[0]) / l_f * _silu(gf_ref[0])


def _fox_sample(q3, kn3, vn3, fn_col, gf3, cache_kf, cache_vf, cache_ft, page_table, layer, group):
    assert H_FOX & (H_FOX - 1) == 0 and group % 2 == 0
    nb, n_pages = page_table.shape
    page = cache_ft.shape[3]
    u = jnp.asarray(np.tril(np.ones((page, page), np.float32), -1), BF16)
    e = jnp.asarray(np.repeat(np.eye(page, dtype=np.float32), H_FOX, axis=1), BF16)
    last = n_pages - 1
    vec = lambda b, j, pt: (b, 0, 0)
    const = lambda b, j, pt: (0, 0)

    def pg(g):
        return lambda b, j, pt, _l=layer, _g=g: (_l, pt[b, last - (j * group + _g)], 0, 0)

    kv_specs = [pl.BlockSpec((1, 1, page * H_FOX, HD_FOX), pg(g)) for g in range(group)]
    f_specs = [pl.BlockSpec((1, 1, H_FOX, page), pg(g)) for g in range(group)]
    head_spec = pl.BlockSpec((1, H_FOX, HD_FOX), vec)
    grid_spec = pltpu.PrefetchScalarGridSpec(
        num_scalar_prefetch=1,
        grid=(nb, n_pages // group),
        in_specs=[head_spec, head_spec, head_spec, pl.BlockSpec((1, H_FOX, 1), vec), head_spec]
        + kv_specs + kv_specs + f_specs + [pl.BlockSpec(u.shape, const), pl.BlockSpec(e.shape, const)],
        out_specs=head_spec,
        scratch_shapes=[
            pltpu.VMEM((H_FOX, 1), F32),
            pltpu.VMEM((H_FOX, 1), F32),
            pltpu.VMEM((H_FOX, HD_FOX), F32),
            pltpu.VMEM((H_FOX, 1), F32),
        ],
    )
    return pl.pallas_call(
        functools.partial(_fox_sample_kernel, group=group),
        name="fox_sample",
        grid_spec=grid_spec,
        out_shape=jax.ShapeDtypeStruct((nb, H_FOX, HD_FOX), F32),
        compiler_params=_cparams("arbitrary", "arbitrary"),
    )(page_table, q3, kn3, vn3, fn_col, gf3, *([cache_kf] * group), *([cache_vf] * group),
      *([cache_ft] * group), u, e)


def _rotary_tables(pos, k_scale):
    half = DK_RET // 2
    inv = ROPE_BASE ** (-np.arange(half, dtype=np.float64) / half)
    ang = np.asarray(pos, np.float64)[:, None] * inv[None, :]
    cos = np.concatenate([np.cos(ang), np.cos(ang)], axis=-1)
    sin = np.concatenate([-np.sin(ang), np.sin(ang)], axis=-1)
    return (jnp.asarray(np.stack([cos, cos * k_scale]), F32),
            jnp.asarray(np.stack([sin, sin * k_scale]), F32))


def _prep_weights(norm_g, w_in, b_forget, q_norm_g, k_norm_g, ret_gn_g, w_branch_ret, w_branch_fox,
                  w_out, w_ple_gate, w_ple_proj):
    bf = lambda a: a.astype(BF16)
    pad = LANES - H_FOX
    return {
        "norm_g": norm_g[:, None, :],
        "q_norm_g": q_norm_g[:, None, :],
        "k_norm_g": k_norm_g[:, None, :],
        "ret_gn_g": ret_gn_g[:, None, :],
        "b_fl": jnp.pad(b_forget.astype(F32), ((0, 0), (0, pad)))[:, None, :],
        "w_a": bf(w_in[:, :, :_FL0]),
        "w_fl": bf(jnp.pad(w_in[:, :, _FL0:_GF0], ((0, 0), (0, 0), (0, pad)))),
        "w_gates": bf(w_in[:, :, _GF0:]),
        "wra": bf(w_branch_ret),
        "wfa": bf(w_branch_fox),
        "wout": bf(w_out),
        "wpg": bf(w_ple_gate),
        "wpp": bf(w_ple_proj),
    }


def kernel(x_prompt, x_sample, cache_k, cache_v, cache_logf, state_ret, page_table, p_prompt, p_sample,
           norm_g, w_in, b_forget, q_norm_g, k_norm_g, ret_gn_g, w_branch_ret, w_branch_fox, w_out,
           w_ple_gate, w_ple_proj):
    batch, seq, d = x_prompt.shape
    nb, tdec, _ = x_sample.shape
    depth, n_pool, page = cache_k.shape[0], cache_k.shape[1], cache_k.shape[2]
    past_len = page_table.shape[1] * page
    n = batch * seq
    ns = nb * tdec
    assert tdec == 1, "decode group handles one new position per sequence"

    tm_p = min(256, seq)
    rb = min(512, seq)
    tq = seq
    tk = min(512, seq)
    tm_c = min(512, seq)

    rot_p = _rotary_tables(np.arange(seq), DK_RET ** -0.5)
    rot_s = _rotary_tables(np.tile(past_len + np.arange(tdec), nb), DK_RET ** -0.5)
    wts = _prep_weights(norm_g, w_in, b_forget, q_norm_g, k_norm_g, ret_gn_g, w_branch_ret,
                        w_branch_fox, w_out, w_ple_gate, w_ple_proj)

    ckf = cache_k.reshape(depth, n_pool, page * H_FOX, HD_FOX)
    cvf = cache_v.reshape(depth, n_pool, page * H_FOX, HD_FOX)
    group = math.gcd(page_table.shape[1], 8)
    cft = jnp.swapaxes(cache_logf, 2, 3)
    p_p = p_prompt.reshape(depth, n, p_prompt.shape[-1])
    p_s = p_sample.reshape(depth, ns, p_sample.shape[-1])
    heads = lambda a: a.reshape(ns, H_FOX, HD_FOX)

    hp = x_prompt.reshape(n, d)
    hs = x_sample.reshape(ns, d)
    prev_p = prev_s = None
    fp_l, sp_l, fs_l, ss_l = [], [], [], []
    for l in range(depth):
        pr = _proj_fused(hp, wts, rot_p, tm=tm_p, layer=l, depth=depth, prev=prev_p, act_dtype=BF16)
        prev_p = (pr["k_stack"], pr["v_stack"])
        logf = pr["logf"]
        fp_l.append(logf[:, :H_FOX])
        fb = _fbias(logf.reshape(batch, seq, LANES))
        a_in, s_fin = _ret_prompt(pr["qk_r"], pr["vg_r"], wts["ret_gn_g"], l, batch, seq, rb)
        sp_l.append(s_fin)
        b_in = _fox_prompt(pr["q_f"], pr["k_f"], fb, pr["v_aug"], pr["gates"], batch, seq, tq, tk)
        hp = _combine(a_in, b_in, pr["gates"], hp, p_p, l, wts, tm_c)
        sr = _proj_fused(hs, wts, rot_s, tm=ns, layer=l, depth=depth, prev=prev_s, act_dtype=F32)
        prev_s = (sr["k_stack"], sr["v_stack"])
        logf_s = sr["logf"][:, :H_FOX]
        fs_l.append(logf_s)
        a_s, s_new = _ret_sample(sr["qk_r"].reshape(ns, 1, -1), sr["vg_r"].reshape(ns, 1, -1),
                                 state_ret, l, wts["ret_gn_g"])
        ss_l.append(s_new)
        v_new = sr["v_aug"].reshape(ns, H_FOX, 2, HD_FOX)[:, :, 0, :]
        b_s = _fox_sample(heads(sr["q_f"]), heads(sr["k_f"]), v_new, logf_s.reshape(ns, H_FOX, 1),
                          heads(sr["gates"][:, :W_FOX]), ckf, cvf, cft, page_table, l, group)
        hs = _combine(a_s.reshape(ns, -1), b_s.reshape(ns, -1), sr["gates"], hs, p_s, l, wts, ns)

    return (
        hp.reshape(batch, seq, d),
        hs.reshape(nb, tdec, d),
        prev_p[0].reshape(depth, batch, seq, H_FOX, HD_FOX),
        prev_p[1].reshape(depth, batch, seq, H_FOX, HD_FOX),
        jnp.stack(fp_l).reshape(depth, batch, seq, H_FOX),
        jnp.stack(sp_l),
        prev_s[0].reshape(depth, nb, tdec, H_FOX, HD_FOX),
        prev_s[1].reshape(depth, nb, tdec, H_FOX, HD_FOX),
        jnp.stack(fs_l).reshape(depth, nb, tdec, H_FOX),
        jnp.stack(ss_l),
    )
```

```python
import functools
import math

import numpy as np
import jax
import jax.numpy as jnp
from jax import lax
from jax.experimental import pallas as pl
from jax.experimental.pallas import tpu as pltpu

F32 = jnp.float32
BF16 = jnp.bfloat16

H_RET = 4
DK_RET = 128
DV_RET = 256
RET_CHUNK = 128
W_RET = H_RET * DV_RET
H_FOX = 8
HD_FOX = 128
W_FOX = H_FOX * HD_FOX
ROPE_BASE = 10000.0
EPS = 1e-6
LOG2E = math.log2(math.e)
LANES = 128
NEG_BIG = -1e30
VMEM_LIMIT = 56 * 1024 * 1024
PROJ_COLS = 512
DECODE_RING = 3

_QR0 = 0
_KR0 = _QR0 + H_RET * DK_RET
_VR0 = _KR0 + H_RET * DK_RET
_GR0 = _VR0 + W_RET
_QF0 = _GR0 + W_RET
_KF0 = _QF0 + W_FOX
_VF0 = _KF0 + W_FOX
_FL0 = _VF0 + W_FOX
_GF0 = _FL0 + H_FOX
_GA0 = _GF0 + W_FOX


def _cparams(*sem):
    return pltpu.CompilerParams(dimension_semantics=sem, vmem_limit_bytes=VMEM_LIMIT)


def _sigmoid(x):
    return 1.0 / (1.0 + jnp.exp(-x))


def _silu(x):
    return x * _sigmoid(x)


def _split3(x):
    hi = x.astype(BF16).astype(F32)
    r1 = x - hi
    mid = r1.astype(BF16).astype(F32)
    lo = (r1 - mid).astype(BF16).astype(F32)
    return hi, mid, lo


def _layer_spec(a, layer, width=None, col=0, resident=False):
    width = a.shape[2] if width is None else width
    return pl.BlockSpec((1, a.shape[1], width), lambda *_, _l=layer, _c=col: (_l, 0, _c),
                        pipeline_mode=pl.Buffered(1) if resident else None)


def _proj_fused_kernel(x_ref, g_ref, cos_ref, sin_ref, hgq_ref, hgk_ref, bfl_ref,
                       wqk_ref, wvr_ref, wgr_ref, wqf_ref, wkf_ref, wvf_ref, wg_ref, wfl_ref, *rest,
                       n_alias):
    (oqk_ref, ovg_ref, oqf_ref, okst_ref, okb_ref, ovst_ref, ovb_ref, olf_ref, og_ref,
     xn_ref) = rest[n_alias:]
    x = x_ref[...]
    ms = jnp.mean(x * x, axis=-1, keepdims=True)
    xn_ref[...] = (x * lax.rsqrt(ms + EPS) * g_ref[0]).astype(BF16)

    def seg(w_ref, c0):
        return jnp.dot(xn_ref[...], w_ref[0, :, c0:c0 + PROJ_COLS], preferred_element_type=F32)

    heads = PROJ_COLS // LANES
    for c0 in range(0, wqk_ref.shape[2], PROJ_COLS):
        z = seg(wqk_ref, c0)
        which = 0 if c0 < H_RET * DK_RET else 1
        cos, sin = cos_ref[which], sin_ref[which]
        for hh in range(heads):
            zh = z[:, hh * LANES:(hh + 1) * LANES]
            rot = pltpu.roll(zh, LANES // 2, axis=1)
            oqk_ref[:, c0 + hh * LANES:c0 + (hh + 1) * LANES] = (zh * cos + rot * sin).astype(oqk_ref.dtype)
    for w_ref, o_ref, o0 in ((wvr_ref, ovg_ref, 0), (wgr_ref, ovg_ref, W_RET), (wg_ref, og_ref, 0)):
        for c0 in range(0, w_ref.shape[2], PROJ_COLS):
            o_ref[:, o0 + c0:o0 + c0 + PROJ_COLS] = seg(w_ref, c0).astype(o_ref.dtype)
    hgq = hgq_ref[0] * (HD_FOX ** -0.5 * LOG2E)
    hgk = hgk_ref[0]
    for c0 in range(0, W_FOX, PROJ_COLS):
        zq = seg(wqf_ref, c0)
        zk = seg(wkf_ref, c0)
        for hh in range(heads):
            cols = slice(c0 + hh * LANES, c0 + (hh + 1) * LANES)
            zh = zq[:, hh * LANES:(hh + 1) * LANES]
            yq = zh * lax.rsqrt(jnp.mean(zh * zh, axis=-1, keepdims=True) + EPS) * hgq
            oqf_ref[:, cols] = yq.astype(oqf_ref.dtype)
            zh = zk[:, hh * LANES:(hh + 1) * LANES]
            yk = zh * lax.rsqrt(jnp.mean(zh * zh, axis=-1, keepdims=True) + EPS) * hgk
            okst_ref[0, :, cols] = yk
            okb_ref[:, cols] = yk.astype(okb_ref.dtype)
    ones = jnp.ones((x.shape[0], LANES), ovb_ref.dtype)
    for c0 in range(0, W_FOX, PROJ_COLS):
        z = seg(wvf_ref, c0)
        ovst_ref[0, :, c0:c0 + PROJ_COLS] = z
        for hh in range(heads):
            h = c0 // LANES + hh
            ovb_ref[:, 2 * h * LANES:(2 * h + 1) * LANES] = z[:, hh * LANES:(hh + 1) * LANES].astype(ovb_ref.dtype)
            ovb_ref[:, (2 * h + 1) * LANES:(2 * h + 2) * LANES] = ones
    z = jnp.dot(xn_ref[...], wfl_ref[0], preferred_element_type=F32) + bfl_ref[0]
    olf_ref[...] = jnp.minimum(z, 0.0) - jnp.log1p(jnp.exp(-jnp.abs(z)))


def _proj_fused(x2d, wts, rot, *, tm, layer, depth, prev, act_dtype):
    m, d = x2d.shape
    cos, sin = rot
    nt = cos.shape[1] // tm
    row = lambda i: (i, 0)
    tab = pl.BlockSpec((2, tm, LANES), lambda i: (0, i % nt, 0))
    lay = functools.partial(_layer_spec, layer=layer)
    wa, wg, wfl = wts["w_a"], wts["w_gates"], wts["w_fl"]
    n_seg = _FL0 // W_FOX
    args = [x2d, wts["norm_g"], cos, sin, wts["q_norm_g"], wts["k_norm_g"], wts["b_fl"]]
    args += [wa] * n_seg + [wg, wfl]
    in_specs = [pl.BlockSpec((tm, d), row), lay(wts["norm_g"]), tab, tab, lay(wts["q_norm_g"]),
                lay(wts["k_norm_g"]), lay(wts["b_fl"])]
    in_specs += [lay(wa, width=W_FOX, col=c, resident=True) for c in range(n_seg)]
    in_specs += [lay(wg, resident=True), lay(wfl, resident=True)]
    aliases = {}
    if prev is not None:
        aliases = {len(args): 3, len(args) + 1: 5}
        args += list(prev)
        in_specs += [pl.BlockSpec(memory_space=pl.ANY)] * 2
    flat = lambda cols, dt: (jax.ShapeDtypeStruct((m, cols), dt), pl.BlockSpec((tm, cols), row))
    stack = lambda cols: (jax.ShapeDtypeStruct((depth, m, cols), F32),
                          pl.BlockSpec((1, tm, cols), lambda i, _l=layer: (_l, i, 0)))
    outs = [flat(2 * H_RET * DK_RET, act_dtype), flat(2 * W_RET, act_dtype), flat(W_FOX, act_dtype),
            stack(W_FOX), flat(W_FOX, act_dtype), stack(W_FOX), flat(2 * W_FOX, act_dtype),
            flat(LANES, F32), flat(3 * d, act_dtype)]
    res = pl.pallas_call(
        functools.partial(_proj_fused_kernel, n_alias=len(aliases)),
        name=f"proj_fused_m{m}",
        grid=(m // tm,),
        in_specs=in_specs,
        out_specs=[o[1] for o in outs],
        out_shape=[o[0] for o in outs],
        scratch_shapes=[pltpu.VMEM((tm, d), BF16)],
        input_output_aliases=aliases,
        compiler_params=_cparams("arbitrary"),
    )(*args)
    names = ("qk_r", "vg_r", "q_f", "k_stack", "k_f", "v_stack", "v_aug", "logf", "gates")
    return dict(zip(names, res))


def _fbias_kernel(x_ref, tri_ref, sel_ref, o_ref):
    c = tri_ref.shape[0]
    tri = tri_ref[...]
    sel = sel_ref[...]
    carry = jnp.zeros((1, x_ref.shape[2]), F32)
    for k in range(x_ref.shape[1] // c):
        rows = slice(k * c, (k + 1) * c)
        hi, mid, lo = _split3(x_ref[0, rows, :])
        f = (jnp.dot(tri, hi.astype(BF16), preferred_element_type=F32)
             + jnp.dot(tri, mid.astype(BF16), preferred_element_type=F32)
             + jnp.dot(tri, lo.astype(BF16), preferred_element_type=F32)) + carry
        pieces = jnp.concatenate(_split3(f * (-LOG2E)), axis=1).astype(BF16)
        o_ref[0, rows, :] = jnp.dot(pieces, sel, preferred_element_type=F32).astype(o_ref.dtype)
        carry = f[c - 1:c, :]


def _fbias(logf3d, chunk=128):
    b, s, w = logf3d.shape
    tri = jnp.asarray(np.tril(np.ones((chunk, chunk), np.float32)), BF16)
    sel = np.zeros((3 * LANES, LANES), np.float32)
    for piece in range(3):
        for h in range(H_FOX):
            sel[piece * LANES + h, piece * H_FOX + h] = 1.0
    sel = jnp.asarray(sel, BF16)
    return pl.pallas_call(
        _fbias_kernel,
        name="fbias",
        grid=(b,),
        in_specs=[pl.BlockSpec((1, s, w), lambda i: (i, 0, 0)),
                  pl.BlockSpec((chunk, chunk), lambda i: (0, 0)),
                  pl.BlockSpec(sel.shape, lambda i: (0, 0))],
        out_specs=pl.BlockSpec((1, s, LANES), lambda i: (i, 0, 0)),
        out_shape=jax.ShapeDtypeStruct((b, s, LANES), BF16),
        compiler_params=_cparams("arbitrary"),
    )(logf3d, tri, sel)


def _groupnorm_gate(o, gain, gate):
    mu = jnp.mean(o, axis=-1, keepdims=True)
    d = o - mu
    var = jnp.mean(d * d, axis=-1, keepdims=True)
    return d * lax.rsqrt(var + EPS) * gain * _silu(gate)


def _ret_prompt_kernel(qk_ref, v_ref, gr_ref, dmat_ref, qdec_ref, kdec_ref, sdec_ref, gn_ref,
                       a_ref, sout_ref, s_ref):
    r = pl.program_id(1)

    @pl.when(r == 0)
    def _():
        s_ref[...] = jnp.zeros_like(s_ref)

    rb = qk_ref.shape[0]
    for c in range(rb // RET_CHUNK):
        rows = slice(c * RET_CHUNK, (c + 1) * RET_CHUNK)
        for h in range(H_RET):
            q = qk_ref[rows, h * DK_RET:(h + 1) * DK_RET]
            k = qk_ref[rows, (H_RET + h) * DK_RET:(H_RET + h + 1) * DK_RET]
            v = v_ref[rows, h * DV_RET:(h + 1) * DV_RET]
            s_prev = s_ref[h]
            sc = lax.dot_general(q, k, (((1,), (1,)), ((), ())), preferred_element_type=F32) * dmat_ref[h]
            inner = jnp.dot(sc.astype(BF16), v, preferred_element_type=F32)
            cross = jnp.dot(q, s_prev.astype(BF16), preferred_element_type=F32) * qdec_ref[h]
            o = inner + cross
            kd = (k.astype(F32) * kdec_ref[h]).astype(BF16)
            s_ref[h] = sdec_ref[h] * s_prev + lax.dot_general(
                kd, v, (((0,), (0,)), ((), ())), preferred_element_type=F32)
            cols = slice(h * DV_RET, (h + 1) * DV_RET)
            a_ref[rows, cols] = _groupnorm_gate(
                o, gn_ref[0, :, cols], gr_ref[rows, cols].astype(F32)).astype(a_ref.dtype)

    @pl.when(r == pl.num_programs(1) - 1)
    def _():
        sout_ref[0] = s_ref[...]


def _ret_tables():
    lg = np.log1p(-np.exp2(-5.0 - np.arange(H_RET, dtype=np.float64)))
    idx = np.arange(RET_CHUNK, dtype=np.float64)
    rel = idx[:, None] - idx[None, :]
    dmat = np.where(rel >= 0, np.exp(lg[:, None, None] * np.maximum(rel, 0.0)[None]), 0.0)
    qdec = np.exp(lg[:, None] * (idx[None, :] + 1.0))
    kdec = np.exp(lg[:, None] * (RET_CHUNK - 1.0 - idx)[None, :])
    sdec = np.exp(lg * RET_CHUNK)
    gamma = np.exp(lg)
    return (jnp.asarray(dmat, F32),
            jnp.asarray(np.broadcast_to(qdec[:, :, None], (H_RET, RET_CHUNK, DV_RET)), F32),
            jnp.asarray(np.broadcast_to(kdec[:, :, None], (H_RET, RET_CHUNK, DK_RET)), F32),
            jnp.asarray(np.broadcast_to(sdec[:, None, None], (H_RET, 1, DV_RET)), F32),
            [float(x) for x in gamma])


def _ret_prompt(qk, vg, gn_g, layer, batch, seq, rb):
    n = qk.shape[0]
    nrb = seq // rb
    dmat, qdec, kdec, sdec, _ = _ret_tables()
    full3 = lambda b, r: (0, 0, 0)
    return pl.pallas_call(
        _ret_prompt_kernel,
        name="ret_prompt",
        grid=(batch, nrb),
        in_specs=[
            pl.BlockSpec((rb, 2 * H_RET * DK_RET), lambda b, r: (b * nrb + r, 0)),
            pl.BlockSpec((rb, W_RET), lambda b, r: (b * nrb + r, 0)),
            pl.BlockSpec((rb, W_RET), lambda b, r: (b * nrb + r, 1)),
            pl.BlockSpec(dmat.shape, full3),
            pl.BlockSpec(qdec.shape, full3),
            pl.BlockSpec(kdec.shape, full3),
            pl.BlockSpec(sdec.shape, full3),
            _layer_spec(gn_g, layer),
        ],
        out_specs=[
            pl.BlockSpec((rb, W_RET), lambda b, r: (b * nrb + r, 0)),
            pl.BlockSpec((1, H_RET, DK_RET, DV_RET), lambda b, r: (b, 0, 0, 0)),
        ],
        out_shape=[
            jax.ShapeDtypeStruct((n, W_RET), BF16),
            jax.ShapeDtypeStruct((batch, H_RET, DK_RET, DV_RET), F32),
        ],
        scratch_shapes=[pltpu.VMEM((H_RET, DK_RET, DV_RET), F32)],
        compiler_params=_cparams("arbitrary", "arbitrary"),
    )(qk, vg, vg, dmat, qdec, kdec, sdec, gn_g)


def _fox_prompt_kernel(q_ref, k_ref, fb_ref, v_ref, gf_ref, o_ref, *, tk, nsplit):
    h = pl.program_id(1)
    i = pl.program_id(2)
    tq = q_ref.shape[0]
    lane = lax.broadcasted_iota(jnp.int32, (tq, LANES), 1)
    mine = (lane == h) | (lane == h + H_FOX) | (lane == h + 2 * H_FOX)
    q = jnp.concatenate([q_ref[...], jnp.where(mine, 1.0, 0.0).astype(q_ref.dtype)], axis=1)
    nsub = tq // tk
    th = tq // nsplit

    def step(off, carry, diag_chunk):
        k = jnp.concatenate([k_ref[pl.ds(off, tk), :], fb_ref[0, pl.ds(off, tk), :]], axis=1)
        v = v_ref[pl.ds(off, tk), :]
        new = []
        for part, (m, acc) in enumerate(carry):
            row0 = part * th
            if diag_chunk is not None and diag_chunk * tk > row0 + th - 1:
                new.append((m, acc))
                continue
            s = lax.dot_general(q[row0:row0 + th], k, (((1,), (1,)), ((), ())),
                                preferred_element_type=F32)
            if diag_chunk is not None and diag_chunk * tk + tk - 1 > row0:
                row = lax.broadcasted_iota(jnp.int32, (th, tk), 0) + row0
                col = lax.broadcasted_iota(jnp.int32, (th, tk), 1) + diag_chunk * tk
                s = jnp.where(col <= row, s, NEG_BIG)
            m_new = jnp.maximum(m, jnp.max(s, axis=-1, keepdims=True))
            alpha = jnp.exp2(m - m_new)
            p = jnp.exp2(s - m_new)
            new.append((m_new, alpha * acc + jnp.dot(p.astype(BF16), v, preferred_element_type=F32)))
        return tuple(new)

    def body(c, carry):
        for cc in range(nsub):
            carry = step(pl.multiple_of((c * nsub + cc) * tk, tk), carry, None)
        return carry

    init = tuple((jnp.full((th, 1), NEG_BIG, F32), jnp.zeros((th, 2 * HD_FOX), F32))
                 for _ in range(nsplit))
    carry = lax.fori_loop(0, i, body, init)
    for cc in range(nsub):
        carry = step(pl.multiple_of((i * nsub + cc) * tk, tk), carry, cc)
    acc = jnp.concatenate([a for _, a in carry], axis=0)
    out = acc[:, :HD_FOX] / acc[:, HD_FOX:]
    o_ref[...] = (out * _silu(gf_ref[...].astype(F32))).astype(o_ref.dtype)


def _fox_prompt(q, k, fb, v_aug, gates, batch, seq, tq, tk):
    n = q.shape[0]
    nq = seq // tq
    return pl.pallas_call(
        functools.partial(_fox_prompt_kernel, tk=tk, nsplit=tq // tk),
        name="fox_prompt",
        grid=(batch, H_FOX, nq),
        in_specs=[
            pl.BlockSpec((tq, HD_FOX), lambda b, h, i: (b * nq + i, h)),
            pl.BlockSpec((seq, HD_FOX), lambda b, h, i: (b, h)),
            pl.BlockSpec((1, seq, LANES), lambda b, h, i: (b, 0, 0)),
            pl.BlockSpec((seq, 2 * HD_FOX), lambda b, h, i: (b, h)),
            pl.BlockSpec((tq, HD_FOX), lambda b, h, i: (b * nq + i, h)),
        ],
        out_specs=pl.BlockSpec((tq, HD_FOX), lambda b, h, i: (b * nq + i, h)),
        out_shape=jax.ShapeDtypeStruct((n, W_FOX), BF16),
        compiler_params=_cparams("arbitrary", "arbitrary", "arbitrary"),
    )(q, k, fb, v_aug, gates)


def _combine_kernel(a_ref, b_ref, ga_ref, gb_ref, h_ref, p_ref, wra_ref, wfa_ref, wout_ref,
                    wpg_ref, wpp_ref, o_ref):
    a = jnp.dot(a_ref[...].astype(BF16), wra_ref[0], preferred_element_type=F32)
    b = jnp.dot(b_ref[...].astype(BF16), wfa_ref[0], preferred_element_type=F32)
    m = _sigmoid(ga_ref[...].astype(F32)) * a + _sigmoid(gb_ref[...].astype(F32)) * b
    h1 = h_ref[...] + jnp.dot(m.astype(BF16), wout_ref[0], preferred_element_type=F32)
    gate = _sigmoid(jnp.dot(h1.astype(BF16), wpg_ref[0], preferred_element_type=F32))
    pe = jnp.dot(p_ref[0].astype(BF16), wpp_ref[0], preferred_element_type=F32)
    o_ref[...] = h1 + gate * pe


def _combine(a_in, b_in, gates, h2d, p3d, layer, wts, tm):
    n, d = h2d.shape
    dp = p3d.shape[-1]
    row = lambda i: (i, 0)
    wspec = lambda name: _layer_spec(wts[name], layer, resident=True)
    names = ("wra", "wfa", "wout", "wpg", "wpp")
    return pl.pallas_call(
        _combine_kernel,
        name=f"combine_m{n}",
        grid=(n // tm,),
        in_specs=[
            pl.BlockSpec((tm, W_RET), row),
            pl.BlockSpec((tm, W_FOX), row),
            pl.BlockSpec((tm, d), lambda i: (i, 1)),
            pl.BlockSpec((tm, d), lambda i: (i, 2)),
            pl.BlockSpec((tm, d), row),
            pl.BlockSpec((1, tm, dp), lambda i, _l=layer: (_l, i, 0)),
        ] + [wspec(k) for k in names],
        out_specs=pl.BlockSpec((tm, d), row),
        out_shape=jax.ShapeDtypeStruct((n, d), F32),
        compiler_params=_cparams("arbitrary"),
    )(a_in, b_in, gates, gates, h2d, p3d, *[wts[k] for k in names])


def _ret_sample_kernel(qk_ref, v_ref, gr_ref, st_ref, gn_ref, a_ref, sout_ref, *, gamma):
    qk = qk_ref[0]
    vv = v_ref[0]
    gg = gr_ref[0]
    row8 = lax.broadcasted_iota(jnp.int32, (8, 1), 0)
    for h in range(H_RET):
        q = qk[:, h * DK_RET:(h + 1) * DK_RET]
        k = qk[:, (H_RET + h) * DK_RET:(H_RET + h + 1) * DK_RET]
        cols = slice(h * DV_RET, (h + 1) * DV_RET)
        v = vv[:, cols]
        s_prev = st_ref[0, 0, h]
        q8 = jnp.broadcast_to(q, (8, DK_RET)).astype(BF16)
        cross = jnp.dot(q8, s_prev.astype(BF16), preferred_element_type=F32)[0:1] * gamma[h]
        qk_dot = jnp.sum(q * k, axis=-1, keepdims=True)
        o = qk_dot * v + cross
        k8 = jnp.where(row8 == 0, jnp.broadcast_to(k, (8, DK_RET)), 0.0).astype(BF16)
        v8 = jnp.broadcast_to(v, (8, DV_RET)).astype(BF16)
        outer = lax.dot_general(k8, v8, (((0,), (0,)), ((), ())), preferred_element_type=F32)
        sout_ref[0, h] = gamma[h] * s_prev + outer
        a_ref[0, :, cols] = _groupnorm_gate(o, gn_ref[0, :, cols], gg[:, cols])


def _ret_sample(qk3, vg3, state, layer, gn_g):
    nb = qk3.shape[0]
    gamma = _ret_tables()[4]
    return pl.pallas_call(
        functools.partial(_ret_sample_kernel, gamma=gamma),
        name="ret_sample",
        grid=(nb,),
        in_specs=[
            pl.BlockSpec((1, 1, 2 * H_RET * DK_RET), lambda b: (b, 0, 0)),
            pl.BlockSpec((1, 1, W_RET), lambda b: (b, 0, 0)),
            pl.BlockSpec((1, 1, W_RET), lambda b: (b, 0, 1)),
            pl.BlockSpec((1, 1, H_RET, DK_RET, DV_RET), lambda b, _l=layer: (_l, b, 0, 0, 0)),
            _layer_spec(gn_g, layer),
        ],
        out_specs=[
            pl.BlockSpec((1, 1, W_RET), lambda b: (b, 0, 0)),
            pl.BlockSpec((1, H_RET, DK_RET, DV_RET), lambda b: (b, 0, 0, 0)),
        ],
        out_shape=[
            jax.ShapeDtypeStruct((nb, 1, W_RET), F32),
            jax.ShapeDtypeStruct((nb, H_RET, DK_RET, DV_RET), F32),
        ],
        compiler_params=_cparams("arbitrary"),
    )(qk3, vg3, vg3, state, gn_g)


def _fox_sample_kernel(pt_ref, q_ref, kn_ref, vn_ref, fn_ref, gf_ref, ck_hbm, cv_hbm, *rest,
                       group, layer, nbuf):
    cf_refs = rest[:group]
    u_ref, e_ref, o_ref, kbuf, vbuf, sem, m_ref, l_ref, acc_ref, c_ref = rest[group:]
    j = pl.program_id(1)
    nsteps = pl.num_programs(1)
    half = group // 2
    last_page = nsteps * group - 1

    t = pl.program_id(0) * nsteps + j
    total = pl.num_programs(0) * nsteps

    def page_copies(step, slot):
        seq_idx = step // nsteps
        first = (step - seq_idx * nsteps) * group
        copies = []
        for g in range(group):
            pidx = pt_ref[seq_idx, last_page - (first + g)]
            copies.append(pltpu.make_async_copy(ck_hbm.at[layer, pidx], kbuf.at[slot, g], sem.at[slot, 0]))
            copies.append(pltpu.make_async_copy(cv_hbm.at[layer, pidx], vbuf.at[slot, g], sem.at[slot, 1]))
        return copies

    @pl.when(t == 0)
    def _():
        for ahead in range(nbuf - 1):
            for cp in page_copies(ahead, ahead):
                cp.start()

    @pl.when(t + (nbuf - 1) < total)
    def _():
        step = t + (nbuf - 1)
        for cp in page_copies(step, step % nbuf):
            cp.start()

    slot = t % nbuf
    for cp in page_copies(t, slot):
        cp.wait()
    ck_pages = [kbuf.at[slot, g] for g in range(group)]
    cv_pages = [vbuf.at[slot, g] for g in range(group)]

    @pl.when(j == 0)
    def _():
        m_ref[...] = jnp.full_like(m_ref, NEG_BIG)
        l_ref[...] = jnp.zeros_like(l_ref)
        acc_ref[...] = jnp.zeros_like(acc_ref)
        c_ref[...] = jnp.zeros_like(c_ref)

    u = u_ref[...]
    carry = c_ref[...]
    pieces = []
    for g in range(group):
        lf = cf_refs[g][0, 0]
        hi, mid, lo = _split3(lf)
        suf = (jnp.dot(hi.astype(BF16), u, preferred_element_type=F32)
               + jnp.dot(mid.astype(BF16), u, preferred_element_type=F32)
               + jnp.dot(lo.astype(BF16), u, preferred_element_type=F32))
        pieces.extend(_split3((suf + (carry + fn_ref[0])) * LOG2E))
        carry = carry + jnp.sum(lf, axis=-1, keepdims=True)
    c_ref[...] = carry
    spread = jnp.dot(jnp.concatenate(pieces, axis=0).astype(BF16), e_ref[...],
                     preferred_element_type=F32)
    bias = [spread[3 * H_FOX * g:3 * H_FOX * g + H_FOX]
            + spread[3 * H_FOX * g + H_FOX:3 * H_FOX * g + 2 * H_FOX]
            + spread[3 * H_FOX * g + 2 * H_FOX:3 * H_FOX * (g + 1)] for g in range(group)]
    bias2 = jnp.concatenate([jnp.concatenate(bias[0::2], axis=1),
                             jnp.concatenate(bias[1::2], axis=1)], axis=0)

    q = q_ref[0]
    zero = jnp.zeros_like(q)
    q2 = jnp.concatenate([jnp.concatenate([q, zero], axis=1),
                          jnp.concatenate([zero, q], axis=1)], axis=0).astype(BF16)
    kall = jnp.concatenate(
        [jnp.concatenate([ck_pages[2 * i][...], ck_pages[2 * i + 1][...]], axis=1) for i in range(half)],
        axis=0).astype(BF16)
    s = lax.dot_general(q2, kall, (((1,), (1,)), ((), ())), preferred_element_type=F32) + bias2
    row = lax.broadcasted_iota(jnp.int32, s.shape, 0)
    col = lax.broadcasted_iota(jnp.int32, s.shape, 1)
    s = jnp.where((col & (H_FOX - 1)) == (row & (H_FOX - 1)), s, NEG_BIG)
    mx = jnp.max(s, axis=-1, keepdims=True)
    m_new = jnp.maximum(m_ref[...], jnp.maximum(mx[:H_FOX], mx[H_FOX:]))
    alpha = jnp.exp2(m_ref[...] - m_new)
    p = jnp.exp2(s - jnp.concatenate([m_new, m_new], axis=0))
    ps = jnp.sum(p, axis=-1, keepdims=True)
    l_ref[...] = alpha * l_ref[...] + ps[:H_FOX] + ps[H_FOX:]
    vall = jnp.concatenate(
        [jnp.concatenate([cv_pages[2 * i][...], cv_pages[2 * i + 1][...]], axis=1) for i in range(half)],
        axis=0).astype(BF16)
    pv = jnp.dot(p.astype(BF16), vall, preferred_element_type=F32)
    acc_ref[...] = alpha * acc_ref[...] + pv[:H_FOX, :HD_FOX] + pv[H_FOX:, HD_FOX:]
    m_ref[...] = m_new

    @pl.when(j == pl.num_programs(1) - 1)
    def _():
        s_new = jnp.sum(q * kn_ref[0], axis=-1, keepdims=True)
        m_f = jnp.maximum(m_ref[...], s_new)
        al = jnp.exp2(m_ref[...] - m_f)
        p_new = jnp.exp2(s_new - m_f)
        l_f = al * l_ref[...] + p_new
        o_ref[0] = (al * acc_ref[...] + p_new * vn_ref[0]) / l_f * _silu(gf_ref[0])


def _fox_sample(q3, kn3, vn3, fn_col, gf3, cache_kf, cache_vf, cache_ft, page_table, layer, group):
    assert H_FOX & (H_FOX - 1) == 0 and group % 2 == 0
    nb, n_pages = page_table.shape
    page = cache_ft.shape[3]
    u = jnp.asarray(np.tril(np.ones((page, page), np.float32), -1), BF16)
    e = jnp.asarray(np.repeat(np.eye(page, dtype=np.float32), H_FOX, axis=1), BF16)
    last = n_pages - 1
    vec = lambda b, j, pt: (b, 0, 0)
    const = lambda b, j, pt: (0, 0)

    def pg(g):
        return lambda b, j, pt, _l=layer, _g=g: (_l, pt[b, last - (j * group + _g)], 0, 0)

    nbuf = DECODE_RING
    assert nb * (n_pages // group) >= nbuf - 1
    hbm = pl.BlockSpec(memory_space=pl.ANY)
    f_specs = [pl.BlockSpec((1, 1, H_FOX, page), pg(g)) for g in range(group)]
    head_spec = pl.BlockSpec((1, H_FOX, HD_FOX), vec)
    grid_spec = pltpu.PrefetchScalarGridSpec(
        num_scalar_prefetch=1,
        grid=(nb, n_pages // group),
        in_specs=[head_spec, head_spec, head_spec, pl.BlockSpec((1, H_FOX, 1), vec), head_spec, hbm, hbm]
        + f_specs + [pl.BlockSpec(u.shape, const), pl.BlockSpec(e.shape, const)],
        out_specs=head_spec,
        scratch_shapes=[
            pltpu.VMEM((nbuf, group, page * H_FOX, HD_FOX), cache_kf.dtype),
            pltpu.VMEM((nbuf, group, page * H_FOX, HD_FOX), cache_vf.dtype),
            pltpu.SemaphoreType.DMA((nbuf, 2)),
            pltpu.VMEM((H_FOX, 1), F32),
            pltpu.VMEM((H_FOX, 1), F32),
            pltpu.VMEM((H_FOX, HD_FOX), F32),
            pltpu.VMEM((H_FOX, 1), F32),
        ],
    )
    return pl.pallas_call(
        functools.partial(_fox_sample_kernel, group=group, layer=layer, nbuf=nbuf),
        name="fox_sample",
        grid_spec=grid_spec,
        out_shape=jax.ShapeDtypeStruct((nb, H_FOX, HD_FOX), F32),
        compiler_params=_cparams("arbitrary", "arbitrary"),
    )(page_table, q3, kn3, vn3, fn_col, gf3, cache_kf, cache_vf, *([cache_ft] * group), u, e)


def _rotary_tables(pos, k_scale):
    half = DK_RET // 2
    inv = ROPE_BASE ** (-np.arange(half, dtype=np.float64) / half)
    ang = np.asarray(pos, np.float64)[:, None] * inv[None, :]
    cos = np.concatenate([np.cos(ang), np.cos(ang)], axis=-1)
    sin = np.concatenate([-np.sin(ang), np.sin(ang)], axis=-1)
    return (jnp.asarray(np.stack([cos, cos * k_scale]), F32),
            jnp.asarray(np.stack([sin, sin * k_scale]), F32))


def _prep_weights(norm_g, w_in, b_forget, q_norm_g, k_norm_g, ret_gn_g, w_branch_ret, w_branch_fox,
                  w_out, w_ple_gate, w_ple_proj):
    bf = lambda a: a.astype(BF16)
    pad = LANES - H_FOX
    w_in = bf(w_in)
    return {
        "norm_g": norm_g[:, None, :],
        "q_norm_g": q_norm_g[:, None, :],
        "k_norm_g": k_norm_g[:, None, :],
        "ret_gn_g": ret_gn_g[:, None, :],
        "b_fl": jnp.pad(b_forget.astype(F32), ((0, 0), (0, pad)))[:, None, :],
        "w_a": w_in,
        "w_fl": jnp.pad(w_in[:, :, _FL0:_GF0], ((0, 0), (0, 0), (0, pad))),
        "w_gates": w_in[:, :, _GF0:],
        "wra": bf(w_branch_ret),
        "wfa": bf(w_branch_fox),
        "wout": bf(w_out),
        "wpg": bf(w_ple_gate),
        "wpp": bf(w_ple_proj),
    }


def kernel(x_prompt, x_sample, cache_k, cache_v, cache_logf, state_ret, page_table, p_prompt, p_sample,
           norm_g, w_in, b_forget, q_norm_g, k_norm_g, ret_gn_g, w_branch_ret, w_branch_fox, w_out,
           w_ple_gate, w_ple_proj):
    batch, seq, d = x_prompt.shape
    nb, tdec, _ = x_sample.shape
    depth, n_pool, page = cache_k.shape[0], cache_k.shape[1], cache_k.shape[2]
    past_len = page_table.shape[1] * page
    n = batch * seq
    ns = nb * tdec
    assert tdec == 1, "decode group handles one new position per sequence"

    tm_p = min(256, seq)
    rb = min(512, seq)
    tq = seq
    tk = min(512, seq)
    tm_c = min(512, seq)

    rot_p = _rotary_tables(np.arange(seq), DK_RET ** -0.5)
    rot_s = _rotary_tables(np.tile(past_len + np.arange(tdec), nb), DK_RET ** -0.5)
    wts = _prep_weights(norm_g, w_in, b_forget, q_norm_g, k_norm_g, ret_gn_g, w_branch_ret,
                        w_branch_fox, w_out, w_ple_gate, w_ple_proj)

    ckf = cache_k.reshape(depth, n_pool, page * H_FOX, HD_FOX)
    cvf = cache_v.reshape(depth, n_pool, page * H_FOX, HD_FOX)
    group = math.gcd(page_table.shape[1], 8)
    cft = jnp.swapaxes(cache_logf, 2, 3)
    p_p = p_prompt.reshape(depth, n, p_prompt.shape[-1])
    p_s = p_sample.reshape(depth, ns, p_sample.shape[-1])
    heads = lambda a: a.reshape(ns, H_FOX, HD_FOX)

    hp = x_prompt.reshape(n, d)
    hs = x_sample.reshape(ns, d)
    prev_p = prev_s = None
    fp_l, sp_l, fs_l, ss_l = [], [], [], []
    for l in range(depth):
        pr = _proj_fused(hp, wts, rot_p, tm=tm_p, layer=l, depth=depth, prev=prev_p, act_dtype=BF16)
        prev_p = (pr["k_stack"], pr["v_stack"])
        logf = pr["logf"]
        fp_l.append(logf[:, :H_FOX])
        fb = _fbias(logf.reshape(batch, seq, LANES))
        a_in, s_fin = _ret_prompt(pr["qk_r"], pr["vg_r"], wts["ret_gn_g"], l, batch, seq, rb)
        sp_l.append(s_fin)
        b_in = _fox_prompt(pr["q_f"], pr["k_f"], fb, pr["v_aug"], pr["gates"], batch, seq, tq, tk)
        hp = _combine(a_in, b_in, pr["gates"], hp, p_p, l, wts, tm_c)
        sr = _proj_fused(hs, wts, rot_s, tm=ns, layer=l, depth=depth, prev=prev_s, act_dtype=F32)
        prev_s = (sr["k_stack"], sr["v_stack"])
        logf_s = sr["logf"][:, :H_FOX]
        fs_l.append(logf_s)
        a_s, s_new = _ret_sample(sr["qk_r"].reshape(ns, 1, -1), sr["vg_r"].reshape(ns, 1, -1),
                                 state_ret, l, wts["ret_gn_g"])
        ss_l.append(s_new)
        v_new = sr["v_aug"].reshape(ns, H_FOX, 2, HD_FOX)[:, :, 0, :]
        b_s = _fox_sample(heads(sr["q_f"]), heads(sr["k_f"]), v_new, logf_s.reshape(ns, H_FOX, 1),
                          heads(sr["gates"][:, :W_FOX]), ckf, cvf, cft, page_table, l, group)
        hs = _combine(a_s.reshape(ns, -1), b_s.reshape(ns, -1), sr["gates"], hs, p_s, l, wts, ns)

    return (
        hp.reshape(batch, seq, d),
        hs.reshape(nb, tdec, d),
        prev_p[0].reshape(depth, batch, seq, H_FOX, HD_FOX),
        prev_p[1].reshape(depth, batch, seq, H_FOX, HD_FOX),
        jnp.stack(fp_l).reshape(depth, batch, seq, H_FOX),
        jnp.stack(sp_l),
        prev_s[0].reshape(depth, nb, tdec, H_FOX, HD_FOX),
        prev_s[1].reshape(depth, nb, tdec, H_FOX, HD_FOX),
        jnp.stack(fs_l).reshape(depth, nb, tdec, H_FOX),
        jnp.stack(ss_l),
    )
```

```python
import functools
import math

import numpy as np
import jax
import jax.numpy as jnp
from jax import lax
from jax.experimental import pallas as pl
from jax.experimental.pallas import tpu as pltpu

F32 = jnp.float32
BF16 = jnp.bfloat16

H_RET = 4
DK_RET = 128
DV_RET = 256
RET_CHUNK = 128
W_RET = H_RET * DV_RET
H_FOX = 8
HD_FOX = 128
W_FOX = H_FOX * HD_FOX
ROPE_BASE = 10000.0
EPS = 1e-6
LOG2E = math.log2(math.e)
LANES = 128
NEG_BIG = -1e30
VMEM_LIMIT = 60 * 1024 * 1024
PROJ_COLS = 512
DECODE_RING = 3

_QR0 = 0
_KR0 = _QR0 + H_RET * DK_RET
_VR0 = _KR0 + H_RET * DK_RET
_GR0 = _VR0 + W_RET
_QF0 = _GR0 + W_RET
_KF0 = _QF0 + W_FOX
_VF0 = _KF0 + W_FOX
_FL0 = _VF0 + W_FOX
_GF0 = _FL0 + H_FOX
_GA0 = _GF0 + W_FOX


def _cparams(*sem):
    return pltpu.CompilerParams(dimension_semantics=sem, vmem_limit_bytes=VMEM_LIMIT)


def _sigmoid(x):
    return 1.0 / (1.0 + jnp.exp(-x))


def _silu(x):
    return x * _sigmoid(x)


def _split3(x):
    hi = x.astype(BF16).astype(F32)
    r1 = x - hi
    mid = r1.astype(BF16).astype(F32)
    lo = (r1 - mid).astype(BF16).astype(F32)
    return hi, mid, lo


def _layer_spec(a, layer, width=None, col=0, resident=False):
    width = a.shape[2] if width is None else width
    return pl.BlockSpec((1, a.shape[1], width), lambda *_, _l=layer, _c=col: (_l, 0, _c),
                        pipeline_mode=pl.Buffered(1) if resident else None)


def _proj_fused_kernel(x_ref, g_ref, cos_ref, sin_ref, hgq_ref, hgk_ref, bfl_ref,
                       wqk_ref, wvr_ref, wgr_ref, wqf_ref, wkf_ref, wvf_ref, wg_ref, wfl_ref, *rest,
                       n_alias):
    (oqk_ref, ovg_ref, oqf_ref, okst_ref, okb_ref, ovst_ref, ovb_ref, olf_ref, og_ref,
     xn_ref) = rest[n_alias:]
    x = x_ref[...]
    ms = jnp.mean(x * x, axis=-1, keepdims=True)
    xn_ref[...] = (x * lax.rsqrt(ms + EPS) * g_ref[0]).astype(BF16)

    def seg(w_ref, c0):
        return jnp.dot(xn_ref[...], w_ref[0, :, c0:c0 + PROJ_COLS], preferred_element_type=F32)

    heads = PROJ_COLS // LANES
    for c0 in range(0, wqk_ref.shape[2], PROJ_COLS):
        z = seg(wqk_ref, c0)
        which = 0 if c0 < H_RET * DK_RET else 1
        cos, sin = cos_ref[which], sin_ref[which]
        for hh in range(heads):
            zh = z[:, hh * LANES:(hh + 1) * LANES]
            rot = pltpu.roll(zh, LANES // 2, axis=1)
            oqk_ref[:, c0 + hh * LANES:c0 + (hh + 1) * LANES] = (zh * cos + rot * sin).astype(oqk_ref.dtype)
    for w_ref, o_ref, o0 in ((wvr_ref, ovg_ref, 0), (wgr_ref, ovg_ref, W_RET), (wg_ref, og_ref, 0)):
        for c0 in range(0, w_ref.shape[2], PROJ_COLS):
            o_ref[:, o0 + c0:o0 + c0 + PROJ_COLS] = seg(w_ref, c0).astype(o_ref.dtype)
    hgq = hgq_ref[0] * (HD_FOX ** -0.5 * LOG2E)
    hgk = hgk_ref[0]
    for c0 in range(0, W_FOX, PROJ_COLS):
        zq = seg(wqf_ref, c0)
        zk = seg(wkf_ref, c0)
        for hh in range(heads):
            cols = slice(c0 + hh * LANES, c0 + (hh + 1) * LANES)
            zh = zq[:, hh * LANES:(hh + 1) * LANES]
            yq = zh * lax.rsqrt(jnp.mean(zh * zh, axis=-1, keepdims=True) + EPS) * hgq
            oqf_ref[:, cols] = yq.astype(oqf_ref.dtype)
            zh = zk[:, hh * LANES:(hh + 1) * LANES]
            yk = zh * lax.rsqrt(jnp.mean(zh * zh, axis=-1, keepdims=True) + EPS) * hgk
            okst_ref[0, :, cols] = yk
            okb_ref[:, cols] = yk.astype(okb_ref.dtype)
    ones = jnp.ones((x.shape[0], LANES), ovb_ref.dtype)
    for c0 in range(0, W_FOX, PROJ_COLS):
        z = seg(wvf_ref, c0)
        ovst_ref[0, :, c0:c0 + PROJ_COLS] = z
        for hh in range(heads):
            h = c0 // LANES + hh
            ovb_ref[:, 2 * h * LANES:(2 * h + 1) * LANES] = z[:, hh * LANES:(hh + 1) * LANES].astype(ovb_ref.dtype)
            ovb_ref[:, (2 * h + 1) * LANES:(2 * h + 2) * LANES] = ones
    z = jnp.dot(xn_ref[...], wfl_ref[0], preferred_element_type=F32) + bfl_ref[0]
    olf_ref[...] = jnp.minimum(z, 0.0) - jnp.log1p(jnp.exp(-jnp.abs(z)))


def _proj_fused(x2d, wts, rot, *, tm, layer, depth, prev, act_dtype):
    m, d = x2d.shape
    cos, sin = rot
    nt = cos.shape[1] // tm
    row = lambda i: (i, 0)
    tab = pl.BlockSpec((2, tm, LANES), lambda i: (0, i % nt, 0))
    lay = functools.partial(_layer_spec, layer=layer)
    wa, wg, wfl = wts["w_a"], wts["w_gates"], wts["w_fl"]
    n_seg = _FL0 // W_FOX
    args = [x2d, wts["norm_g"], cos, sin, wts["q_norm_g"], wts["k_norm_g"], wts["b_fl"]]
    args += [wa] * n_seg + [wg, wfl]
    in_specs = [pl.BlockSpec((tm, d), row), lay(wts["norm_g"]), tab, tab, lay(wts["q_norm_g"]),
                lay(wts["k_norm_g"]), lay(wts["b_fl"])]
    in_specs += [lay(wa, width=W_FOX, col=c, resident=True) for c in range(n_seg)]
    in_specs += [lay(wg, resident=True), lay(wfl, resident=True)]
    aliases = {}
    if prev is not None:
        aliases = {len(args): 3, len(args) + 1: 5}
        args += list(prev)
        in_specs += [pl.BlockSpec(memory_space=pl.ANY)] * 2
    flat = lambda cols, dt: (jax.ShapeDtypeStruct((m, cols), dt), pl.BlockSpec((tm, cols), row))
    stack = lambda cols: (jax.ShapeDtypeStruct((depth, m, cols), F32),
                          pl.BlockSpec((1, tm, cols), lambda i, _l=layer: (_l, i, 0)))
    outs = [flat(2 * H_RET * DK_RET, act_dtype), flat(2 * W_RET, act_dtype), flat(W_FOX, act_dtype),
            stack(W_FOX), flat(W_FOX, act_dtype), stack(W_FOX), flat(2 * W_FOX, act_dtype),
            flat(LANES, F32), flat(3 * d, act_dtype)]
    res = pl.pallas_call(
        functools.partial(_proj_fused_kernel, n_alias=len(aliases)),
        name=f"proj_fused_m{m}",
        grid=(m // tm,),
        in_specs=in_specs,
        out_specs=[o[1] for o in outs],
        out_shape=[o[0] for o in outs],
        scratch_shapes=[pltpu.VMEM((tm, d), BF16)],
        input_output_aliases=aliases,
        compiler_params=_cparams("arbitrary"),
    )(*args)
    names = ("qk_r", "vg_r", "q_f", "k_stack", "k_f", "v_stack", "v_aug", "logf", "gates")
    return dict(zip(names, res))


def _fbias_kernel(x_ref, tri_ref, sel_ref, o_ref):
    c = tri_ref.shape[0]
    tri = tri_ref[...]
    sel = sel_ref[...]
    carry = jnp.zeros((1, x_ref.shape[2]), F32)
    for k in range(x_ref.shape[1] // c):
        rows = slice(k * c, (k + 1) * c)
        hi, mid, lo = _split3(x_ref[0, rows, :])
        f = (jnp.dot(tri, hi.astype(BF16), preferred_element_type=F32)
             + jnp.dot(tri, mid.astype(BF16), preferred_element_type=F32)
             + jnp.dot(tri, lo.astype(BF16), preferred_element_type=F32)) + carry
        pieces = jnp.concatenate(_split3(f * (-LOG2E)), axis=1).astype(BF16)
        o_ref[0, rows, :] = jnp.dot(pieces, sel, preferred_element_type=F32).astype(o_ref.dtype)
        carry = f[c - 1:c, :]


def _fbias(logf3d, chunk=128):
    b, s, w = logf3d.shape
    tri = jnp.asarray(np.tril(np.ones((chunk, chunk), np.float32)), BF16)
    sel = np.zeros((3 * LANES, LANES), np.float32)
    for piece in range(3):
        for h in range(H_FOX):
            sel[piece * LANES + h, piece * H_FOX + h] = 1.0
    sel = jnp.asarray(sel, BF16)
    return pl.pallas_call(
        _fbias_kernel,
        name="fbias",
        grid=(b,),
        in_specs=[pl.BlockSpec((1, s, w), lambda i: (i, 0, 0)),
                  pl.BlockSpec((chunk, chunk), lambda i: (0, 0)),
                  pl.BlockSpec(sel.shape, lambda i: (0, 0))],
        out_specs=pl.BlockSpec((1, s, LANES), lambda i: (i, 0, 0)),
        out_shape=jax.ShapeDtypeStruct((b, s, LANES), BF16),
        compiler_params=_cparams("arbitrary"),
    )(logf3d, tri, sel)


def _groupnorm_gate(o, gain, gate):
    mu = jnp.mean(o, axis=-1, keepdims=True)
    d = o - mu
    var = jnp.mean(d * d, axis=-1, keepdims=True)
    return d * lax.rsqrt(var + EPS) * gain * _silu(gate)


def _ret_prompt_kernel(qk_ref, v_ref, gr_ref, dmat_ref, qdec_ref, kdec_ref, sdec_ref, gn_ref,
                       a_ref, sout_ref, s_ref):
    r = pl.program_id(1)

    @pl.when(r == 0)
    def _():
        s_ref[...] = jnp.zeros_like(s_ref)

    rb = qk_ref.shape[0]
    for c in range(rb // RET_CHUNK):
        rows = slice(c * RET_CHUNK, (c + 1) * RET_CHUNK)
        for h in range(H_RET):
            q = qk_ref[rows, h * DK_RET:(h + 1) * DK_RET]
            k = qk_ref[rows, (H_RET + h) * DK_RET:(H_RET + h + 1) * DK_RET]
            v = v_ref[rows, h * DV_RET:(h + 1) * DV_RET]
            s_prev = s_ref[h]
            sc = lax.dot_general(q, k, (((1,), (1,)), ((), ())), preferred_element_type=F32) * dmat_ref[h]
            inner = jnp.dot(sc.astype(BF16), v, preferred_element_type=F32)
            cross = jnp.dot(q, s_prev.astype(BF16), preferred_element_type=F32) * qdec_ref[h]
            o = inner + cross
            kd = (k.astype(F32) * kdec_ref[h]).astype(BF16)
            s_ref[h] = sdec_ref[h] * s_prev + lax.dot_general(
                kd, v, (((0,), (0,)), ((), ())), preferred_element_type=F32)
            cols = slice(h * DV_RET, (h + 1) * DV_RET)
            a_ref[rows, cols] = _groupnorm_gate(
                o, gn_ref[0, :, cols], gr_ref[rows, cols].astype(F32)).astype(a_ref.dtype)

    @pl.when(r == pl.num_programs(1) - 1)
    def _():
        sout_ref[0] = s_ref[...]


def _ret_tables():
    lg = np.log1p(-np.exp2(-5.0 - np.arange(H_RET, dtype=np.float64)))
    idx = np.arange(RET_CHUNK, dtype=np.float64)
    rel = idx[:, None] - idx[None, :]
    dmat = np.where(rel >= 0, np.exp(lg[:, None, None] * np.maximum(rel, 0.0)[None]), 0.0)
    qdec = np.exp(lg[:, None] * (idx[None, :] + 1.0))
    kdec = np.exp(lg[:, None] * (RET_CHUNK - 1.0 - idx)[None, :])
    sdec = np.exp(lg * RET_CHUNK)
    gamma = np.exp(lg)
    return (jnp.asarray(dmat, F32),
            jnp.asarray(np.broadcast_to(qdec[:, :, None], (H_RET, RET_CHUNK, DV_RET)), F32),
            jnp.asarray(np.broadcast_to(kdec[:, :, None], (H_RET, RET_CHUNK, DK_RET)), F32),
            jnp.asarray(np.broadcast_to(sdec[:, None, None], (H_RET, 1, DV_RET)), F32),
            [float(x) for x in gamma])


def _ret_prompt(qk, vg, gn_g, layer, batch, seq, rb):
    n = qk.shape[0]
    nrb = seq // rb
    dmat, qdec, kdec, sdec, _ = _ret_tables()
    full3 = lambda b, r: (0, 0, 0)
    return pl.pallas_call(
        _ret_prompt_kernel,
        name="ret_prompt",
        grid=(batch, nrb),
        in_specs=[
            pl.BlockSpec((rb, 2 * H_RET * DK_RET), lambda b, r: (b * nrb + r, 0)),
            pl.BlockSpec((rb, W_RET), lambda b, r: (b * nrb + r, 0)),
            pl.BlockSpec((rb, W_RET), lambda b, r: (b * nrb + r, 1)),
            pl.BlockSpec(dmat.shape, full3),
            pl.BlockSpec(qdec.shape, full3),
            pl.BlockSpec(kdec.shape, full3),
            pl.BlockSpec(sdec.shape, full3),
            _layer_spec(gn_g, layer),
        ],
        out_specs=[
            pl.BlockSpec((rb, W_RET), lambda b, r: (b * nrb + r, 0)),
            pl.BlockSpec((1, H_RET, DK_RET, DV_RET), lambda b, r: (b, 0, 0, 0)),
        ],
        out_shape=[
            jax.ShapeDtypeStruct((n, W_RET), BF16),
            jax.ShapeDtypeStruct((batch, H_RET, DK_RET, DV_RET), F32),
        ],
        scratch_shapes=[pltpu.VMEM((H_RET, DK_RET, DV_RET), F32)],
        compiler_params=_cparams("arbitrary", "arbitrary"),
    )(qk, vg, vg, dmat, qdec, kdec, sdec, gn_g)


def _fox_prompt_kernel(q_ref, k_ref, fb_ref, v_ref, gf_ref, o_ref, *, tk, nsplit):
    h = pl.program_id(1)
    i = pl.program_id(2)
    tq = q_ref.shape[0]
    lane = lax.broadcasted_iota(jnp.int32, (tq, LANES), 1)
    mine = (lane == h) | (lane == h + H_FOX) | (lane == h + 2 * H_FOX)
    q = jnp.concatenate([q_ref[...], jnp.where(mine, 1.0, 0.0).astype(q_ref.dtype)], axis=1)
    nsub = tq // tk
    th = tq // nsplit

    def step(off, carry, diag_chunk):
        k = jnp.concatenate([k_ref[pl.ds(off, tk), :], fb_ref[0, pl.ds(off, tk), :]], axis=1)
        v = v_ref[pl.ds(off, tk), :]

        def update(rows, cols, m, acc, masked):
            (r0, nr), (c0, nc) = rows, cols
            s = lax.dot_general(q[r0:r0 + nr], k[c0:c0 + nc], (((1,), (1,)), ((), ())),
                                preferred_element_type=F32)
            if masked:
                row = lax.broadcasted_iota(jnp.int32, (nr, nc), 0) + r0
                col = lax.broadcasted_iota(jnp.int32, (nr, nc), 1) + (diag_chunk * tk + c0)
                s = jnp.where(col <= row, s, NEG_BIG)
            m_new = jnp.maximum(m, jnp.max(s, axis=-1, keepdims=True))
            alpha = jnp.exp2(m - m_new)
            p = jnp.exp2(s - m_new)
            return m_new, alpha * acc + jnp.dot(p.astype(BF16), v[c0:c0 + nc], preferred_element_type=F32)

        new = []
        for part, (m, acc) in enumerate(carry):
            row0 = part * th
            if diag_chunk is not None and diag_chunk * tk > row0 + th - 1:
                new.append((m, acc))
            elif diag_chunk is None or diag_chunk * tk + tk - 1 <= row0:
                new.append(update((row0, th), (0, tk), m, acc, False))
            else:
                new.append(update((row0, th), (0, tk), m, acc, True))
        return tuple(new)

    def body(c, carry):
        for cc in range(nsub):
            carry = step(pl.multiple_of((c * nsub + cc) * tk, tk), carry, None)
        return carry

    init = tuple((jnp.full((th, 1), NEG_BIG, F32), jnp.zeros((th, 2 * HD_FOX), F32))
                 for _ in range(nsplit))
    carry = lax.fori_loop(0, i, body, init)
    for cc in range(nsub):
        carry = step(pl.multiple_of((i * nsub + cc) * tk, tk), carry, cc)
    acc = jnp.concatenate([a for _, a in carry], axis=0)
    out = acc[:, :HD_FOX] / acc[:, HD_FOX:]
    o_ref[...] = (out * _silu(gf_ref[...].astype(F32))).astype(o_ref.dtype)


def _fox_prompt(q, k, fb, v_aug, gates, batch, seq, tq, tk):
    n = q.shape[0]
    nq = seq // tq
    return pl.pallas_call(
        functools.partial(_fox_prompt_kernel, tk=tk, nsplit=tq // tk),
        name="fox_prompt",
        grid=(batch, H_FOX, nq),
        in_specs=[
            pl.BlockSpec((tq, HD_FOX), lambda b, h, i: (b * nq + i, h)),
            pl.BlockSpec((seq, HD_FOX), lambda b, h, i: (b, h)),
            pl.BlockSpec((1, seq, LANES), lambda b, h, i: (b, 0, 0)),
            pl.BlockSpec((seq, 2 * HD_FOX), lambda b, h, i: (b, h)),
            pl.BlockSpec((tq, HD_FOX), lambda b, h, i: (b * nq + i, h)),
        ],
        out_specs=pl.BlockSpec((tq, HD_FOX), lambda b, h, i: (b * nq + i, h)),
        out_shape=jax.ShapeDtypeStruct((n, W_FOX), BF16),
        compiler_params=_cparams("arbitrary", "arbitrary", "arbitrary"),
    )(q, k, fb, v_aug, gates)


def _combine_kernel(a_ref, b_ref, ga_ref, gb_ref, h_ref, p_ref, wra_ref, wfa_ref, wout_ref,
                    wpg_ref, wpp_ref, o_ref):
    a = jnp.dot(a_ref[...].astype(BF16), wra_ref[0], preferred_element_type=F32)
    b = jnp.dot(b_ref[...].astype(BF16), wfa_ref[0], preferred_element_type=F32)
    m = _sigmoid(ga_ref[...].astype(F32)) * a + _sigmoid(gb_ref[...].astype(F32)) * b
    h1 = h_ref[...] + jnp.dot(m.astype(BF16), wout_ref[0], preferred_element_type=F32)
    gate = _sigmoid(jnp.dot(h1.astype(BF16), wpg_ref[0], preferred_element_type=F32))
    pe = jnp.dot(p_ref[0].astype(BF16), wpp_ref[0], preferred_element_type=F32)
    o_ref[...] = h1 + gate * pe


def _combine(a_in, b_in, gates, h2d, p3d, layer, wts, tm):
    n, d = h2d.shape
    dp = p3d.shape[-1]
    row = lambda i: (i, 0)
    wspec = lambda name: _layer_spec(wts[name], layer, resident=True)
    names = ("wra", "wfa", "wout", "wpg", "wpp")
    return pl.pallas_call(
        _combine_kernel,
        name=f"combine_m{n}",
        grid=(n // tm,),
        in_specs=[
            pl.BlockSpec((tm, W_RET), row),
            pl.BlockSpec((tm, W_FOX), row),
            pl.BlockSpec((tm, d), lambda i: (i, 1)),
            pl.BlockSpec((tm, d), lambda i: (i, 2)),
            pl.BlockSpec((tm, d), row),
            pl.BlockSpec((1, tm, dp), lambda i, _l=layer: (_l, i, 0)),
        ] + [wspec(k) for k in names],
        out_specs=pl.BlockSpec((tm, d), row),
        out_shape=jax.ShapeDtypeStruct((n, d), F32),
        compiler_params=_cparams("arbitrary"),
    )(a_in, b_in, gates, gates, h2d, p3d, *[wts[k] for k in names])


def _ret_sample_kernel(qk_ref, v_ref, gr_ref, st_ref, gn_ref, a_ref, sout_ref, *, gamma):
    row8 = lax.broadcasted_iota(jnp.int32, (8, 1), 0)
    for si in range(qk_ref.shape[0]):
        qk = qk_ref[si]
        vv = v_ref[si]
        gg = gr_ref[si]
        for h in range(H_RET):
            q = qk[:, h * DK_RET:(h + 1) * DK_RET]
            k = qk[:, (H_RET + h) * DK_RET:(H_RET + h + 1) * DK_RET]
            cols = slice(h * DV_RET, (h + 1) * DV_RET)
            v = vv[:, cols]
            s_prev = st_ref[0, si, h]
            q8 = jnp.broadcast_to(q, (8, DK_RET)).astype(BF16)
            cross = jnp.dot(q8, s_prev.astype(BF16), preferred_element_type=F32)[0:1] * gamma[h]
            qk_dot = jnp.sum(q * k, axis=-1, keepdims=True)
            o = qk_dot * v + cross
            k8 = jnp.where(row8 == 0, jnp.broadcast_to(k, (8, DK_RET)), 0.0).astype(BF16)
            v8 = jnp.broadcast_to(v, (8, DV_RET)).astype(BF16)
            outer = lax.dot_general(k8, v8, (((0,), (0,)), ((), ())), preferred_element_type=F32)
            sout_ref[si, h] = gamma[h] * s_prev + outer
            a_ref[si, :, cols] = _groupnorm_gate(o, gn_ref[0, :, cols], gg[:, cols])


def _ret_sample(qk3, vg3, state, layer, gn_g):
    nb = qk3.shape[0]
    sb = math.gcd(nb, 4)
    gamma = _ret_tables()[4]
    return pl.pallas_call(
        functools.partial(_ret_sample_kernel, gamma=gamma),
        name="ret_sample",
        grid=(nb // sb,),
        in_specs=[
            pl.BlockSpec((sb, 1, 2 * H_RET * DK_RET), lambda b: (b, 0, 0)),
            pl.BlockSpec((sb, 1, W_RET), lambda b: (b, 0, 0)),
            pl.BlockSpec((sb, 1, W_RET), lambda b: (b, 0, 1)),
            pl.BlockSpec((1, sb, H_RET, DK_RET, DV_RET), lambda b, _l=layer: (_l, b, 0, 0, 0)),
            _layer_spec(gn_g, layer),
        ],
        out_specs=[
            pl.BlockSpec((sb, 1, W_RET), lambda b: (b, 0, 0)),
            pl.BlockSpec((sb, H_RET, DK_RET, DV_RET), lambda b: (b, 0, 0, 0)),
        ],
        out_shape=[
            jax.ShapeDtypeStruct((nb, 1, W_RET), F32),
            jax.ShapeDtypeStruct((nb, H_RET, DK_RET, DV_RET), F32),
        ],
        compiler_params=_cparams("arbitrary"),
    )(qk3, vg3, vg3, state, gn_g)


def _fox_sample_kernel(pt_ref, q_ref, kn_ref, vn_ref, fn_ref, gf_ref, ck_hbm, cv_hbm, *rest,
                       group, layer, nbuf):
    cf_refs = rest[:group]
    u_ref, e_ref, o_ref, kbuf, vbuf, sem, m_ref, l_ref, acc_ref, c_ref = rest[group:]
    j = pl.program_id(1)
    nsteps = pl.num_programs(1)
    half = group // 2
    last_page = nsteps * group - 1

    t = pl.program_id(0) * nsteps + j
    total = pl.num_programs(0) * nsteps

    def page_copies(step, slot):
        seq_idx = step // nsteps
        first = (step - seq_idx * nsteps) * group
        copies = []
        for g in range(group):
            pidx = pt_ref[seq_idx, last_page - (first + g)]
            copies.append(pltpu.make_async_copy(ck_hbm.at[layer, pidx], kbuf.at[slot, g], sem.at[slot, 0]))
            copies.append(pltpu.make_async_copy(cv_hbm.at[layer, pidx], vbuf.at[slot, g], sem.at[slot, 1]))
        return copies

    @pl.when(t == 0)
    def _():
        for ahead in range(nbuf - 1):
            for cp in page_copies(ahead, ahead):
                cp.start()

    @pl.when(t + (nbuf - 1) < total)
    def _():
        step = t + (nbuf - 1)
        for cp in page_copies(step, step % nbuf):
            cp.start()

    slot = t % nbuf
    for cp in page_copies(t, slot):
        cp.wait()
    ck_pages = [kbuf.at[slot, g] for g in range(group)]
    cv_pages = [vbuf.at[slot, g] for g in range(group)]

    @pl.when(j == 0)
    def _():
        m_ref[...] = jnp.full_like(m_ref, NEG_BIG)
        l_ref[...] = jnp.zeros_like(l_ref)
        acc_ref[...] = jnp.zeros_like(acc_ref)
        c_ref[...] = jnp.zeros_like(c_ref)

    u = u_ref[...]
    carry = c_ref[...]
    pieces = []
    for g in range(group):
        lf = cf_refs[g][0, 0]
        hi, mid, lo = _split3(lf)
        suf = (jnp.dot(hi.astype(BF16), u, preferred_element_type=F32)
               + jnp.dot(mid.astype(BF16), u, preferred_element_type=F32)
               + jnp.dot(lo.astype(BF16), u, preferred_element_type=F32))
        pieces.extend(_split3((suf + (carry + fn_ref[0])) * LOG2E))
        carry = carry + jnp.sum(lf, axis=-1, keepdims=True)
    c_ref[...] = carry
    spread = jnp.dot(jnp.concatenate(pieces, axis=0).astype(BF16), e_ref[...],
                     preferred_element_type=F32)
    bias = [spread[3 * H_FOX * g:3 * H_FOX * g + H_FOX]
            + spread[3 * H_FOX * g + H_FOX:3 * H_FOX * g + 2 * H_FOX]
            + spread[3 * H_FOX * g + 2 * H_FOX:3 * H_FOX * (g + 1)] for g in range(group)]
    bias2 = jnp.concatenate([jnp.concatenate(bias[0::2], axis=1),
                             jnp.concatenate(bias[1::2], axis=1)], axis=0)

    q = q_ref[0]
    zero = jnp.zeros_like(q)
    q2 = jnp.concatenate([jnp.concatenate([q, zero], axis=1),
                          jnp.concatenate([zero, q], axis=1)], axis=0).astype(BF16)
    kall = jnp.concatenate(
        [jnp.concatenate([ck_pages[2 * i][...], ck_pages[2 * i + 1][...]], axis=1) for i in range(half)],
        axis=0).astype(BF16)
    s = lax.dot_general(q2, kall, (((1,), (1,)), ((), ())), preferred_element_type=F32) + bias2
    row = lax.broadcasted_iota(jnp.int32, s.shape, 0)
    col = lax.broadcasted_iota(jnp.int32, s.shape, 1)
    s = jnp.where((col & (H_FOX - 1)) == (row & (H_FOX - 1)), s, NEG_BIG)
    mx = jnp.max(s, axis=-1, keepdims=True)
    m_new = jnp.maximum(m_ref[...], jnp.maximum(mx[:H_FOX], mx[H_FOX:]))
    alpha = jnp.exp2(m_ref[...] - m_new)
    p = jnp.exp2(s - jnp.concatenate([m_new, m_new], axis=0))
    ps = jnp.sum(p, axis=-1, keepdims=True)
    l_ref[...] = alpha * l_ref[...] + ps[:H_FOX] + ps[H_FOX:]
    vall = jnp.concatenate(
        [jnp.concatenate([cv_pages[2 * i][...], cv_pages[2 * i + 1][...]], axis=1) for i in range(half)],
        axis=0).astype(BF16)
    pv = jnp.dot(p.astype(BF16), vall, preferred_element_type=F32)
    acc_ref[...] = alpha * acc_ref[...] + pv[:H_FOX, :HD_FOX] + pv[H_FOX:, HD_FOX:]
    m_ref[...] = m_new

    @pl.when(j == pl.num_programs(1) - 1)
    def _():
        s_new = jnp.sum(q * kn_ref[0], axis=-1, keepdims=True)
        m_f = jnp.maximum(m_ref[...], s_new)
        al = jnp.exp2(m_ref[...] - m_f)
        p_new = jnp.exp2(s_new - m_f)
        l_f = al * l_ref[...] + p_new
        o_ref[0] = (al * acc_ref[...] + p_new * vn_ref[0]) / l_f * _silu(gf_ref[0])


def _fox_sample(q3, kn3, vn3, fn_col, gf3, cache_kf, cache_vf, cache_ft, page_table, layer, group):
    assert H_FOX & (H_FOX - 1) == 0 and group % 2 == 0
    nb, n_pages = page_table.shape
    page = cache_ft.shape[3]
    u = jnp.asarray(np.tril(np.ones((page, page), np.float32), -1), BF16)
    e = jnp.asarray(np.repeat(np.eye(page, dtype=np.float32), H_FOX, axis=1), BF16)
    last = n_pages - 1
    vec = lambda b, j, pt: (b, 0, 0)
    const = lambda b, j, pt: (0, 0)

    def pg(g):
        return lambda b, j, pt, _l=layer, _g=g: (_l, pt[b, last - (j * group + _g)], 0, 0)

    nbuf = DECODE_RING
    assert nb * (n_pages // group) >= nbuf - 1
    hbm = pl.BlockSpec(memory_space=pl.ANY)
    f_specs = [pl.BlockSpec((1, 1, H_FOX, page), pg(g)) for g in range(group)]
    head_spec = pl.BlockSpec((1, H_FOX, HD_FOX), vec)
    grid_spec = pltpu.PrefetchScalarGridSpec(
        num_scalar_prefetch=1,
        grid=(nb, n_pages // group),
        in_specs=[head_spec, head_spec, head_spec, pl.BlockSpec((1, H_FOX, 1), vec), head_spec, hbm, hbm]
        + f_specs + [pl.BlockSpec(u.shape, const), pl.BlockSpec(e.shape, const)],
        out_specs=head_spec,
        scratch_shapes=[
            pltpu.VMEM((nbuf, group, page * H_FOX, HD_FOX), cache_kf.dtype),
            pltpu.VMEM((nbuf, group, page * H_FOX, HD_FOX), cache_vf.dtype),
            pltpu.SemaphoreType.DMA((nbuf, 2)),
            pltpu.VMEM((H_FOX, 1), F32),
            pltpu.VMEM((H_FOX, 1), F32),
            pltpu.VMEM((H_FOX, HD_FOX), F32),
            pltpu.VMEM((H_FOX, 1), F32),
        ],
    )
    return pl.pallas_call(
        functools.partial(_fox_sample_kernel, group=group, layer=layer, nbuf=nbuf),
        name="fox_sample",
        grid_spec=grid_spec,
        out_shape=jax.ShapeDtypeStruct((nb, H_FOX, HD_FOX), F32),
        compiler_params=_cparams("arbitrary", "arbitrary"),
    )(page_table, q3, kn3, vn3, fn_col, gf3, cache_kf, cache_vf, *([cache_ft] * group), u, e)


def _rotary_tables(pos, k_scale):
    half = DK_RET // 2
    inv = ROPE_BASE ** (-np.arange(half, dtype=np.float64) / half)
    ang = np.asarray(pos, np.float64)[:, None] * inv[None, :]
    cos = np.concatenate([np.cos(ang), np.cos(ang)], axis=-1)
    sin = np.concatenate([-np.sin(ang), np.sin(ang)], axis=-1)
    return (jnp.asarray(np.stack([cos, cos * k_scale]), F32),
            jnp.asarray(np.stack([sin, sin * k_scale]), F32))


def _prep_weights(norm_g, w_in, b_forget, q_norm_g, k_norm_g, ret_gn_g, w_branch_ret, w_branch_fox,
                  w_out, w_ple_gate, w_ple_proj):
    bf = lambda a: a.astype(BF16)
    pad = LANES - H_FOX
    w_in = bf(w_in)
    return {
        "norm_g": norm_g[:, None, :],
        "q_norm_g": q_norm_g[:, None, :],
        "k_norm_g": k_norm_g[:, None, :],
        "ret_gn_g": ret_gn_g[:, None, :],
        "b_fl": jnp.pad(b_forget.astype(F32), ((0, 0), (0, pad)))[:, None, :],
        "w_a": w_in,
        "w_fl": jnp.pad(w_in[:, :, _FL0:_GF0], ((0, 0), (0, 0), (0, pad))),
        "w_gates": w_in[:, :, _GF0:],
        "wra": bf(w_branch_ret),
        "wfa": bf(w_branch_fox),
        "wout": bf(w_out),
        "wpg": bf(w_ple_gate),
        "wpp": bf(w_ple_proj),
    }


def kernel(x_prompt, x_sample, cache_k, cache_v, cache_logf, state_ret, page_table, p_prompt, p_sample,
           norm_g, w_in, b_forget, q_norm_g, k_norm_g, ret_gn_g, w_branch_ret, w_branch_fox, w_out,
           w_ple_gate, w_ple_proj):
    batch, seq, d = x_prompt.shape
    nb, tdec, _ = x_sample.shape
    depth, n_pool, page = cache_k.shape[0], cache_k.shape[1], cache_k.shape[2]
    past_len = page_table.shape[1] * page
    n = batch * seq
    ns = nb * tdec
    assert tdec == 1, "decode group handles one new position per sequence"

    tm_p = min(512, seq)
    rb = min(512, seq)
    tq = seq
    tk = min(512, seq)
    tm_c = min(512, seq)

    rot_p = _rotary_tables(np.arange(seq), DK_RET ** -0.5)
    rot_s = _rotary_tables(np.tile(past_len + np.arange(tdec), nb), DK_RET ** -0.5)
    wts = _prep_weights(norm_g, w_in, b_forget, q_norm_g, k_norm_g, ret_gn_g, w_branch_ret,
                        w_branch_fox, w_out, w_ple_gate, w_ple_proj)

    ckf = cache_k.reshape(depth, n_pool, page * H_FOX, HD_FOX)
    cvf = cache_v.reshape(depth, n_pool, page * H_FOX, HD_FOX)
    group = math.gcd(page_table.shape[1], 8)
    cft = jnp.swapaxes(cache_logf, 2, 3)
    p_p = p_prompt.reshape(depth, n, p_prompt.shape[-1])
    p_s = p_sample.reshape(depth, ns, p_sample.shape[-1])
    heads = lambda a: a.reshape(ns, H_FOX, HD_FOX)

    hp = x_prompt.reshape(n, d)
    hs = x_sample.reshape(ns, d)
    prev_p = prev_s = None
    fp_l, sp_l, fs_l, ss_l = [], [], [], []
    for l in range(depth):
        pr = _proj_fused(hp, wts, rot_p, tm=tm_p, layer=l, depth=depth, prev=prev_p, act_dtype=BF16)
        prev_p = (pr["k_stack"], pr["v_stack"])
        logf = pr["logf"]
        fp_l.append(logf[:, :H_FOX])
        fb = _fbias(logf.reshape(batch, seq, LANES))
        a_in, s_fin = _ret_prompt(pr["qk_r"], pr["vg_r"], wts["ret_gn_g"], l, batch, seq, rb)
        sp_l.append(s_fin)
        b_in = _fox_prompt(pr["q_f"], pr["k_f"], fb, pr["v_aug"], pr["gates"], batch, seq, tq, tk)
        hp = _combine(a_in, b_in, pr["gates"], hp, p_p, l, wts, tm_c)
        sr = _proj_fused(hs, wts, rot_s, tm=ns, layer=l, depth=depth, prev=prev_s, act_dtype=F32)
        prev_s = (sr["k_stack"], sr["v_stack"])
        logf_s = sr["logf"][:, :H_FOX]
        fs_l.append(logf_s)
        a_s, s_new = _ret_sample(sr["qk_r"].reshape(ns, 1, -1), sr["vg_r"].reshape(ns, 1, -1),
                                 state_ret, l, wts["ret_gn_g"])
        ss_l.append(s_new)
        v_new = sr["v_aug"].reshape(ns, H_FOX, 2, HD_FOX)[:, :, 0, :]
        b_s = _fox_sample(heads(sr["q_f"]), heads(sr["k_f"]), v_new, logf_s.reshape(ns, H_FOX, 1),
                          heads(sr["gates"][:, :W_FOX]), ckf, cvf, cft, page_table, l, group)
        hs = _combine(a_s.reshape(ns, -1), b_s.reshape(ns, -1), sr["gates"], hs, p_s, l, wts, ns)

    return (
        hp.reshape(batch, seq, d),
        hs.reshape(nb, tdec, d),
        prev_p[0].reshape(depth, batch, seq, H_FOX, HD_FOX),
        prev_p[1].reshape(depth, batch, seq, H_FOX, HD_FOX),
        jnp.stack(fp_l).reshape(depth, batch, seq, H_FOX),
        jnp.stack(sp_l),
        prev_s[0].reshape(depth, nb, tdec, H_FOX, HD_FOX),
        prev_s[1].reshape(depth, nb, tdec, H_FOX, HD_FOX),
        jnp.stack(fs_l).reshape(depth, nb, tdec, H_FOX),
        jnp.stack(ss_l),
    )
```
